```python
import math
import jax
import jax.numpy as jnp
from jax import lax
import numpy as np

D_MODEL = 1024
BATCH = 16
SEQ = 4096
DEPTH = 1
DEC_BATCH = 128
DEC_SEQ = 8
PAST_LEN = 8192
PAGE_SIZE = 128

D_MIX = D_MODEL
A_WIDTH = D_MIX // 2
A_HEADS = 4
A_DH = A_WIDTH // (2 * A_HEADS)
A_VD = 2 * A_DH
B_WIDTH = D_MIX - A_WIDTH
B_HEADS = 8
B_DH = B_WIDTH // B_HEADS
B_KV = 2
B_REP = B_HEADS // B_KV
CMP_STRIDE = 16
CMP_RATIO = 2
CMP_LEN = CMP_STRIDE * CMP_RATIO
CMP_HID = 2 * B_DH
SEL_BLOCK = 64
N_SEL = 16
WINDOW = 512
A_QBLOCK = 128
B_QBLOCK = 16
PEER_HEADS = 8
PEER_NKEYS = 128
PEER_EXPERTS = PEER_NKEYS * PEER_NKEYS
PEER_DK = 256
PEER_TOPK = 16
PEER_CHUNK = 128
IN_WIDTH = 2 * A_HEADS * 2 * A_DH + A_HEADS * A_VD + B_HEADS * B_DH + 3 * 2 * B_KV * B_DH + 3 * B_HEADS
NORM_EPS = 1e-6
NEG = -1e30
FORCED = 1e9

kernel_name = 'hymba_diffattn_nsa_peer_step'


def _rmsnorm(x, g):
    xf = x.astype(jnp.float32)
    y = xf * lax.rsqrt(jnp.mean(xf * xf, axis=-1, keepdims=True) + NORM_EPS)
    return (y * g.astype(jnp.float32)).astype(x.dtype)


def _alibi(n):
    return jnp.asarray(2.0 ** (-8.0 * np.arange(1, n + 1) / n), dtype=jnp.float32)


def _in_proj(xn, w_in):
    sizes = (A_HEADS * 2 * A_DH, A_HEADS * 2 * A_DH, A_HEADS * A_VD, B_HEADS * B_DH,
             2 * B_KV * B_DH, 2 * B_KV * B_DH, 2 * B_KV * B_DH, 3 * B_HEADS)
    h = xn @ w_in
    parts, off = [], 0
    for sz in sizes:
        parts.append(h[..., off:off + sz])
        off += sz
    b, t = xn.shape[:2]
    a_q = parts[0].reshape(b, t, A_HEADS, 2, A_DH)
    a_kv = jnp.concatenate([parts[1].reshape(b, t, A_HEADS, 2 * A_DH),
                            parts[2].reshape(b, t, A_HEADS, A_VD)], axis=-1)
    b_q = parts[3].reshape(b, t, B_KV, B_REP, B_DH)
    c_kv = parts[4].reshape(b, t, B_KV, 2 * B_DH)
    s_kv = parts[5].reshape(b, t, B_KV, 2 * B_DH)
    w_kv = parts[6].reshape(b, t, B_KV, 2 * B_DH)
    gates = jax.nn.sigmoid(parts[7].astype(jnp.float32)).reshape(b, t, B_KV, B_REP, 3).astype(xn.dtype)
    return a_q, a_kv, b_q, c_kv, s_kv, w_kv, gates


def _diff_attn_prompt(q, kv, lam, slopes):
    b, s = q.shape[:2]
    k = kv[..., :2 * A_DH].reshape(b, s, A_HEADS, 2, A_DH)
    v = kv[..., 2 * A_DH:]
    kpos = jnp.arange(s)

    def block(i):
        q0 = i * A_QBLOCK
        qb = lax.dynamic_slice_in_dim(q, q0, A_QBLOCK, axis=1)
        qpos = q0 + jnp.arange(A_QBLOCK)
        sc = jnp.einsum('bqhcd,bkhcd->bhcqk', qb, k).astype(jnp.float32) * (A_DH ** -0.5)
        dist = (qpos[:, None] - kpos[None, :]).astype(jnp.float32)
        sc = jnp.where(dist >= 0, sc - slopes[None, :, None, None, None] * dist, NEG)
        p = jax.nn.softmax(sc, axis=-1)
        attn = p[:, :, 0] - lam * p[:, :, 1]
        return jnp.einsum('bhqk,bkhe->bqhe', attn.astype(v.dtype), v)

    o = lax.map(block, jnp.arange(s // A_QBLOCK))
    return jnp.moveaxis(o, 0, 1).reshape(b, s, A_HEADS, A_VD)


def _diff_attn_sample(q, kv_new, cache_a, layer, page_table, lam, slopes):
    db, nq = q.shape[:2]
    n_pages = page_table.shape[1]
    qpos = n_pages * PAGE_SIZE + jnp.arange(nq)

    def scores(rows, kpos):
        k = rows[..., :2 * A_DH].reshape(db, -1, A_HEADS, 2, A_DH)
        sc = jnp.einsum('bqhcd,bkhcd->bhcqk', q, k).astype(jnp.float32) * (A_DH ** -0.5)
        dist = (qpos[:, None] - kpos[None, :]).astype(jnp.float32)
        sc = jnp.where(dist >= 0, sc - slopes[None, :, None, None, None] * dist, NEG)
        return sc, rows[..., 2 * A_DH:]

    def update(carry, sc, v):
        m, l, acc = carry
        m_new = jnp.maximum(m, sc.max(axis=-1))
        corr = jnp.exp(m - m_new)
        p = jnp.exp(sc - m_new[..., None])
        l = l * corr + p.sum(axis=-1)
        acc = acc * corr[..., None] + jnp.einsum('bhcqk,bkhe->bhcqe', p, v.astype(jnp.float32))
        return (m_new, l, acc)

    def page_step(carry, j):
        rows = cache_a[layer, page_table[:, j]]
        sc, v = scores(rows, j * PAGE_SIZE + jnp.arange(PAGE_SIZE))
        return update(carry, sc, v), None

    init = (jnp.full((db, A_HEADS, 2, nq), NEG, jnp.float32),
            jnp.zeros((db, A_HEADS, 2, nq), jnp.float32),
            jnp.zeros((db, A_HEADS, 2, nq, A_VD), jnp.float32))
    carry, _ = lax.scan(page_step, init, jnp.arange(n_pages))
    sc, v = scores(kv_new, qpos)
    m, l, acc = update(carry, sc, v)
    o = acc / l[..., None]
    out = o[:, :, 0] - lam * o[:, :, 1]
    return out.transpose(0, 2, 1, 3).astype(q.dtype)


def _n_cmp(t):
    return (t - CMP_LEN) // CMP_STRIDE + 1


def _chunk_proj(kv_rows, cmp_pe, cmp_w1):
    lead = kv_rows.shape[:-3]
    n_ch = kv_rows.shape[-3] // CMP_STRIDE
    c = kv_rows.reshape(lead + (n_ch, CMP_STRIDE, B_KV, 2, B_DH))
    pe = cmp_pe.reshape(2, CMP_RATIO, CMP_STRIDE, B_DH).transpose(1, 2, 0, 3)
    w1 = cmp_w1.reshape(2, CMP_RATIO, CMP_STRIDE, B_DH, CMP_HID).transpose(1, 2, 0, 3, 4)
    pe_term = jnp.einsum('rsed,rsedh->reh', pe, w1)
    return jnp.einsum('...nsged,rsedh->...nrgeh', c, w1) + pe_term[:, None]


def _compress(p_chunks, n_cmp, cmp_w2):
    h = p_chunks[:, 0:n_cmp, 0]
    for r in range(1, CMP_RATIO):
        h = h + p_chunks[:, r:r + n_cmp, r]
    ckv = jnp.einsum('bngeh,ehd->bnged', jax.nn.gelu(h, approximate=False), cmp_w2)
    return ckv[..., 0, :], ckv[..., 1, :]


def _cmp_positions(n_cmp):
    return jnp.arange(n_cmp) * CMP_STRIDE + CMP_LEN - 1


def _overlap(n_cmp, n_slc):
    start = (jnp.arange(n_cmp) * CMP_STRIDE)[:, None]
    j = jnp.arange(n_slc)[None, :]
    return ((start < (j + 1) * SEL_BLOCK) & (start + CMP_LEN > j * SEL_BLOCK)).astype(jnp.float32)


def _attend_shared(q, k, v, qpos, kpos, slopes, mask):
    sc = jnp.einsum('bqgrd,bkgd->bgrqk', q, k).astype(jnp.float32) * (B_DH ** -0.5)
    dist = (qpos[:, None] - kpos[None, :]).astype(jnp.float32)
    sc = jnp.where(mask, sc - slopes[None, :, :, None, None] * dist, NEG)
    p = jax.nn.softmax(sc, axis=-1)
    p = jnp.where(mask.any(axis=-1)[:, None], p, 0.0)
    return jnp.einsum('bgrqk,bkgd->bqgrd', p.astype(v.dtype), v), p


def _attend_gathered(q, segments, qpos, slopes):
    scores = []
    for k, _, kpos, mask in segments:
        sc = jnp.einsum('bqgrd,bgqkd->bgrqk', q, k).astype(jnp.float32) * (B_DH ** -0.5)
        dist = (qpos[:, None] - kpos).astype(jnp.float32)
        sc = sc - slopes[None, :, :, None, None] * dist[:, :, None]
        scores.append(jnp.where((mask & (dist >= 0))[:, :, None], sc, NEG))
    p = jax.nn.softmax(jnp.concatenate(scores, axis=-1), axis=-1)
    out, off = 0.0, 0
    for (_, v, _, _), sc in zip(segments, scores):
        n = sc.shape[-1]
        out = out + jnp.einsum('bgrqk,bgqkd->bqgrd', p[..., off:off + n].astype(v.dtype), v)
        off += n
    return out


def _select_blocks(p_cmp, overlap, qpos, n_slc):
    imp = jnp.einsum('bgrqn,nj->bgqj', p_cmp, overlap)
    j = jnp.arange(n_slc)[None, :]
    jcur = (qpos // SEL_BLOCK)[:, None]
    forced = (j == 0) | (j == jcur) | (j == jcur - 1)
    score = jnp.where(j > jcur, -FORCED, jnp.where(forced, FORCED, imp))
    _, idx = lax.top_k(score, min(N_SEL, n_slc))
    return idx


def _nsa_prompt(q, c_kv, s_kv, w_kv, gates, cmp_pe, cmp_w1, cmp_w2, slopes):
    b, s = q.shape[:2]
    n_cmp, n_slc = _n_cmp(s), s // SEL_BLOCK
    ck, cv = _compress(_chunk_proj(c_kv, cmp_pe, cmp_w1), n_cmp, cmp_w2)
    cpos = _cmp_positions(n_cmp)
    ov = _overlap(n_cmp, n_slc)
    sel_blocks = s_kv.reshape(b, n_slc, SEL_BLOCK, B_KV, 2 * B_DH).transpose(0, 3, 1, 2, 4)
    win_pad = jnp.pad(w_kv, ((0, 0), (WINDOW, 0), (0, 0), (0, 0)))
    bi = jnp.arange(b)[:, None, None, None]
    gi = jnp.arange(B_KV)[None, :, None, None]

    def block(i):
        q0 = i * B_QBLOCK
        qb = lax.dynamic_slice_in_dim(q, q0, B_QBLOCK, axis=1)
        gb = lax.dynamic_slice_in_dim(gates, q0, B_QBLOCK, axis=1)
        qpos = q0 + jnp.arange(B_QBLOCK)
        o_c, p_c = _attend_shared(qb, ck, cv, qpos, cpos, slopes, cpos[None, :] <= qpos[:, None])
        idx = _select_blocks(p_c, ov, qpos, n_slc)
        g_kv = sel_blocks[bi, gi, idx].reshape(b, B_KV, B_QBLOCK, -1, 2 * B_DH)
        kpos = (idx[..., None] * SEL_BLOCK + jnp.arange(SEL_BLOCK)).reshape(b, B_KV, B_QBLOCK, -1)
        o_s = _attend_gathered(qb, [(g_kv[..., :B_DH], g_kv[..., B_DH:], kpos, jnp.ones(kpos.shape, bool))],
                               qpos, slopes)
        wkv = lax.dynamic_slice_in_dim(win_pad, q0, WINDOW + B_QBLOCK, axis=1)
        wpos = q0 - WINDOW + jnp.arange(WINDOW + B_QBLOCK)
        d = qpos[:, None] - wpos[None, :]
        o_w, _ = _attend_shared(qb, wkv[..., :B_DH], wkv[..., B_DH:], qpos, wpos, slopes,
                                (d >= 0) & (d < WINDOW) & (wpos[None, :] >= 0))
        return gb[..., 0:1] * o_c + gb[..., 1:2] * o_s + gb[..., 2:3] * o_w

    o = lax.map(block, jnp.arange(s // B_QBLOCK))
    return jnp.moveaxis(o, 0, 1).reshape(b, s, B_HEADS, B_DH)


def _nsa_sample(q, c_new, s_new, w_new, gates, cache_cmp, cache_sel, win_buf, layer, page_table,
                cmp_pe, cmp_w1, cmp_w2, slopes):
    db, nq = q.shape[:2]
    n_pages = page_table.shape[1]
    past = n_pages * PAGE_SIZE
    total = past + nq
    qpos = past + jnp.arange(nq)
    p_past = lax.map(lambda j: _chunk_proj(cache_cmp[layer, page_table[:, j]], cmp_pe, cmp_w1),
                     jnp.arange(n_pages))
    p_past = jnp.moveaxis(p_past, 0, 1)
    p_past = p_past.reshape((db, -1) + p_past.shape[3:])
    n_new_ch = -(-nq // CMP_STRIDE)
    c_pad = jnp.pad(c_new, ((0, 0), (0, n_new_ch * CMP_STRIDE - nq), (0, 0), (0, 0)))
    p_chunks = jnp.concatenate([p_past, _chunk_proj(c_pad, cmp_pe, cmp_w1)], axis=1)
    n_cmp = _n_cmp(total)
    ck, cv = _compress(p_chunks, n_cmp, cmp_w2)
    cpos = _cmp_positions(n_cmp)
    o_c, p_c = _attend_shared(q, ck, cv, qpos, cpos, slopes, cpos[None, :] <= qpos[:, None])
    n_slc = -(-total // SEL_BLOCK)
    n_past_blk = past // SEL_BLOCK
    idx = _select_blocks(p_c, _overlap(n_cmp, n_slc), qpos, n_slc)
    blk_pp = PAGE_SIZE // SEL_BLOCK
    idx_past = jnp.minimum(idx, n_past_blk - 1)
    phys = page_table[jnp.arange(db)[:, None, None, None], idx_past // blk_pp]
    rows = (idx_past % blk_pp)[..., None] * SEL_BLOCK + jnp.arange(SEL_BLOCK)
    gi = jnp.arange(B_KV)[None, :, None, None, None]
    g_kv = cache_sel[layer, phys[..., None], rows, gi].reshape(db, B_KV, nq, -1, 2 * B_DH)
    kpos_past = (idx[..., None] * SEL_BLOCK + jnp.arange(SEL_BLOCK)).reshape(db, B_KV, nq, -1)
    mask_past = jnp.broadcast_to((idx < n_past_blk)[..., None], idx.shape + (SEL_BLOCK,)).reshape(db, B_KV, nq, -1)
    npos = past + jnp.arange(nq)
    mask_new = jnp.any(idx[:, :, :, None, :] == (npos // SEL_BLOCK)[:, None], axis=-1)
    new_rows = jnp.broadcast_to(s_new.transpose(0, 2, 1, 3)[:, :, None], (db, B_KV, nq, nq, 2 * B_DH))
    kpos_new = jnp.broadcast_to(npos, (db, B_KV, nq, nq))
    o_s = _attend_gathered(q, [(g_kv[..., :B_DH], g_kv[..., B_DH:], kpos_past, mask_past),
                               (new_rows[..., :B_DH], new_rows[..., B_DH:], kpos_new, mask_new)],
                           qpos, slopes)
    wb = win_buf.shape[1]
    wkv = jnp.concatenate([win_buf, w_new], axis=1)
    wpos = past - wb + jnp.arange(wb + nq)
    d = qpos[:, None] - wpos[None, :]
    o_w, _ = _attend_shared(q, wkv[..., :B_DH], wkv[..., B_DH:], qpos, wpos, slopes, (d >= 0) & (d < WINDOW))
    o = gates[..., 0:1] * o_c + gates[..., 1:2] * o_s + gates[..., 2:3] * o_w
    return o.reshape(db, nq, B_HEADS, B_DH), wkv[:, -wb:]


def _merge_heads(oa, ob, subln, lam_init, out_norm, w_out):
    b, t = oa.shape[:2]
    oa = _rmsnorm(oa, subln) * (1.0 - lam_init)
    ob = _rmsnorm(ob, out_norm)
    return jnp.concatenate([oa.reshape(b, t, A_WIDTH), ob.reshape(b, t, B_WIDTH)], axis=-1) @ w_out


def _peer(x, w_q, sub_keys, u, v):
    lead = x.shape[:-1]
    xf = x.reshape(-1, D_MODEL)
    n = xf.shape[0]
    n_chunks = -(-n // PEER_CHUNK)
    xf = jnp.pad(xf, ((0, n_chunks * PEER_CHUNK - n), (0, 0)))

    def chunk(xc):
        c = xc.shape[0]
        q = (xc @ w_q).reshape(c, PEER_HEADS, 2, PEER_DK // 2)
        sc = jnp.einsum('chpk,pnk->chpn', q, sub_keys).astype(jnp.float32)
        hv, hi = lax.top_k(sc, PEER_TOPK)
        comb = (hv[:, :, 0, :, None] + hv[:, :, 1, None, :]).reshape(c, PEER_HEADS, PEER_TOPK * PEER_TOPK)
        top, ci = lax.top_k(comb, PEER_TOPK)
        i1 = jnp.take_along_axis(hi[:, :, 0], ci // PEER_TOPK, axis=-1)
        i2 = jnp.take_along_axis(hi[:, :, 1], ci % PEER_TOPK, axis=-1)
        eid = i1 * PEER_NKEYS + i2
        g = jax.nn.softmax(top, axis=-1)
        hdn = jax.nn.gelu(jnp.einsum('chkd,cd->chk', u[eid], xc).astype(jnp.float32), approximate=False)
        return jnp.einsum('chk,chkd->cd', (g * hdn).astype(v.dtype), v[eid])

    out = lax.map(chunk, xf.reshape(n_chunks, PEER_CHUNK, D_MODEL))
    return out.reshape(-1, D_MODEL)[:n].reshape(lead + (D_MODEL,))


def setup_inputs(seed: int = 0) -> dict:
    key = jax.random.key(seed)
    ks = jax.random.split(key, 24)
    f32 = jnp.float32
    nrm = jax.random.normal
    n_pages = PAST_LEN // PAGE_SIZE
    n_phys = (DEC_BATCH * n_pages * 5) // 4
    win_buf = min(WINDOW, PAST_LEN)
    page_table = jax.random.permutation(ks[6], n_phys)[:DEC_BATCH * n_pages].reshape(DEC_BATCH, n_pages).astype(jnp.int32)

    def gain(k, shape):
        return 1.0 + 0.05 * nrm(k, shape, f32)

    return {
        'x_prompt': nrm(ks[0], (BATCH, SEQ, D_MODEL), f32),
        'x_sample': nrm(ks[1], (DEC_BATCH, DEC_SEQ, D_MODEL), f32),
        'cache_a': nrm(ks[2], (DEPTH, n_phys, PAGE_SIZE, A_HEADS, 4 * A_DH), f32),
        'cache_cmp': nrm(ks[3], (DEPTH, n_phys, PAGE_SIZE, B_KV, 2 * B_DH), f32),
        'cache_sel': nrm(ks[4], (DEPTH, n_phys, PAGE_SIZE, B_KV, 2 * B_DH), f32),
        'state_win': nrm(ks[5], (DEPTH, DEC_BATCH, win_buf, B_KV, 2 * B_DH), f32),
        'page_table': page_table,
        'norm_mix': gain(ks[7], (DEPTH, D_MODEL)),
        'w_in': nrm(ks[8], (DEPTH, D_MODEL, IN_WIDTH), f32) * D_MODEL ** -0.5,
        'diff_lambda': 0.1 * nrm(ks[9], (DEPTH, 4, A_DH), f32),
        'diff_subln': gain(ks[10], (DEPTH, A_VD)),
        'cmp_pe': 0.1 * nrm(ks[11], (DEPTH, 2, CMP_LEN, B_DH), f32),
        'cmp_w1': nrm(ks[12], (DEPTH, 2, CMP_LEN * B_DH, CMP_HID), f32) * (CMP_LEN * B_DH) ** -0.5,
        'cmp_w2': nrm(ks[13], (DEPTH, 2, CMP_HID, B_DH), f32) * CMP_HID ** -0.5,
        'nsa_out_norm': gain(ks[14], (DEPTH, B_DH)),
        'w_out': nrm(ks[15], (DEPTH, D_MIX, D_MODEL), f32) * D_MIX ** -0.5,
        'norm_ffn': gain(ks[16], (DEPTH, D_MODEL)),
        'peer_wq': nrm(ks[17], (DEPTH, D_MODEL, PEER_HEADS * PEER_DK), f32) * D_MODEL ** -0.5,
        'peer_subkeys': nrm(ks[18], (DEPTH, 2, PEER_NKEYS, PEER_DK // 2), f32) * (PEER_DK // 2) ** -0.5,
        'peer_u': nrm(ks[19], (DEPTH, PEER_EXPERTS, D_MODEL), f32) * D_MODEL ** -0.5,
        'peer_v': 0.2 * nrm(ks[20], (DEPTH, PEER_EXPERTS, D_MODEL), f32),
        'norm_final': gain(ks[21], (D_MODEL,)),
    }


def reference(x_prompt, x_sample, cache_a, cache_cmp, cache_sel, state_win, page_table,
              norm_mix, w_in, diff_lambda, diff_subln, cmp_pe, cmp_w1, cmp_w2, nsa_out_norm,
              w_out, norm_ffn, peer_wq, peer_subkeys, peer_u, peer_v, norm_final):
    slopes_a = _alibi(A_HEADS)
    slopes_b = _alibi(B_HEADS).reshape(B_KV, B_REP)
    xp, xs = x_prompt, x_sample
    na_p, na_s, nc_p, nc_s, ns_p, ns_s, nw_p, nw_s = [], [], [], [], [], [], [], []
    for l in range(DEPTH):
        lam_init = 0.8 - 0.6 * math.exp(-0.3 * l)
        dl = diff_lambda[l].astype(jnp.float32)
        lam = jnp.exp(jnp.sum(dl[0] * dl[1])) - jnp.exp(jnp.sum(dl[2] * dl[3])) + lam_init
        a_q, a_kv, b_q, c_kv, s_kv, w_kv, gates = _in_proj(_rmsnorm(xp, norm_mix[l]), w_in[l])
        oa = _diff_attn_prompt(a_q, a_kv, lam, slopes_a)
        ob = _nsa_prompt(b_q, c_kv, s_kv, w_kv, gates, cmp_pe[l], cmp_w1[l], cmp_w2[l], slopes_b)
        xp = xp + _merge_heads(oa, ob, diff_subln[l], lam_init, nsa_out_norm[l], w_out[l])
        xp = xp + _peer(_rmsnorm(xp, norm_ffn[l]), peer_wq[l], peer_subkeys[l], peer_u[l], peer_v[l])
        na_p.append(a_kv)
        nc_p.append(c_kv)
        ns_p.append(s_kv)
        nw_p.append(w_kv[:, -min(WINDOW, w_kv.shape[1]):])
        a_q, a_kv, b_q, c_kv, s_kv, w_kv, gates = _in_proj(_rmsnorm(xs, norm_mix[l]), w_in[l])
        oa = _diff_attn_sample(a_q, a_kv, cache_a, l, page_table, lam, slopes_a)
        ob, new_win = _nsa_sample(b_q, c_kv, s_kv, w_kv, gates, cache_cmp, cache_sel, state_win[l], l,
                                  page_table, cmp_pe[l], cmp_w1[l], cmp_w2[l], slopes_b)
        xs = xs + _merge_heads(oa, ob, diff_subln[l], lam_init, nsa_out_norm[l], w_out[l])
        xs = xs + _peer(_rmsnorm(xs, norm_ffn[l]), peer_wq[l], peer_subkeys[l], peer_u[l], peer_v[l])
        na_s.append(a_kv)
        nc_s.append(c_kv)
        ns_s.append(s_kv)
        nw_s.append(new_win)
    y_prompt = _rmsnorm(xp, norm_final)
    y_sample = _rmsnorm(xs, norm_final)
    return (y_prompt, y_sample, jnp.stack(na_p), jnp.stack(na_s), jnp.stack(nc_p), jnp.stack(nc_s),
            jnp.stack(ns_p), jnp.stack(ns_s), jnp.stack(nw_p), jnp.stack(nw_s))
```

```python
import functools
import math

import jax
import jax.numpy as jnp
import numpy as np
from jax import lax
from jax.experimental import pallas as pl
from jax.experimental.pallas import tpu as pltpu

F32 = jnp.float32
BF16 = jnp.bfloat16
I32 = jnp.int32

D_MODEL = 1024
A_HEADS = 4
A_DH = 64
A_VD = 128
B_HEADS = 8
B_DH = 64
B_KV = 2
B_REP = 4
CMP_STRIDE = 16
CMP_LEN = 32
CMP_HID = 128
SEL_BLOCK = 64
N_SEL = 16
WINDOW = 512
PEER_HEADS = 8
PEER_NKEYS = 128
PEER_DK = 256
PEER_TOPK = 16
NORM_EPS = 1e-6
NEG = -1e30
FORCED = 1e9

VMEM_LIMIT_V7X = 52 * 1024 * 1024
LANES = 128

_C_AQ = 0
_C_AKV = 512
_C_BQ = 1536
_C_CKV = 2560
_C_SKV = 2816
_C_WKV = 3072
_C_GATE = 3328
_C_END = 3456


def _nt(a, b):
    return lax.dot_general(a, b, (((1,), (1,)), ((), ())), preferred_element_type=F32)


def _gelu(x):
    return 0.5 * x * (1.0 + lax.erf(x * 0.7071067811865476))


def _params(sem):
    return pltpu.CompilerParams(dimension_semantics=sem, vmem_limit_bytes=VMEM_LIMIT_V7X)


def _alibi(n):
    return np.asarray(2.0 ** (-8.0 * np.arange(1, n + 1) / n), dtype=np.float32)


def _prep_w_in(w_in):
    d = w_in.shape[0]
    aq = w_in[:, 0:512] * 0.125
    ak = w_in[:, 512:1024].reshape(d, A_HEADS, 2 * A_DH)
    av = w_in[:, 1024:1536].reshape(d, A_HEADS, A_VD)
    akv = jnp.concatenate([ak, av], axis=-1).reshape(d, 1024)
    bq = w_in[:, 1536:2048].reshape(d, B_HEADS, B_DH) * 0.125
    bq = jnp.concatenate([jnp.zeros_like(bq), bq], axis=-1).reshape(d, 1024)
    ckv = w_in[:, 2048:2304]
    skv = w_in[:, 2304:2560]
    wkv = w_in[:, 2560:2816]
    gt = w_in[:, 2816:2840].reshape(d, B_HEADS, 3).transpose(0, 2, 1).reshape(d, 24)
    gt = jnp.pad(gt, ((0, 0), (0, LANES - 24)))
    return jnp.concatenate([aq, akv, bq, ckv, skv, wkv, gt], axis=1).astype(BF16)


def _inproj_kernel(x_ref, g_ref, w_ref, aq_ref, akv_ref, akvb_ref, bq_ref, ckv_ref, skv_ref, wkv_ref,
                   skvb_ref, wkvb_ref, gate_ref):
    x = x_ref[...]
    xn = x * lax.rsqrt(jnp.mean(x * x, axis=-1, keepdims=True) + NORM_EPS) * g_ref[...]
    xb = xn.astype(BF16)

    def seg(a, b):
        return jnp.dot(xb, w_ref[:, a:b], preferred_element_type=F32)

    aq_ref[...] = seg(_C_AQ, _C_AKV).astype(BF16)
    akv = seg(_C_AKV, _C_BQ)
    akv_ref[...] = akv
    akvb_ref[...] = akv.astype(BF16)
    bq = seg(_C_BQ, _C_CKV).astype(BF16)
    for hd in range(B_HEADS):
        bq_ref[hd] = bq[:, hd * LANES:(hd + 1) * LANES]
    ckv_ref[...] = seg(_C_CKV, _C_SKV)
    for (a, fref, bref) in ((_C_SKV, skv_ref, skvb_ref), (_C_WKV, wkv_ref, wkvb_ref)):
        kv = seg(a, a + 2 * LANES)
        fref[...] = kv
        sw = jnp.concatenate([pltpu.roll(kv[:, g * LANES:(g + 1) * LANES], 64, 1) for g in range(B_KV)], axis=1)
        bref[...] = sw.astype(BF16)
    gate_ref[...] = jax.nn.sigmoid(seg(_C_GATE, _C_END))


def _in_proj(x, gain, w):
    t = x.shape[0]
    tm = 256
    assert t % tm == 0
    row = lambda n: pl.BlockSpec((tm, n), lambda i: (i, 0))
    out_shape = (
        jax.ShapeDtypeStruct((t, 512), BF16),
        jax.ShapeDtypeStruct((t, 1024), F32),
        jax.ShapeDtypeStruct((t, 1024), BF16),
        jax.ShapeDtypeStruct((B_HEADS, t, LANES), BF16),
        jax.ShapeDtypeStruct((t, 256), F32),
        jax.ShapeDtypeStruct((t, 256), F32),
        jax.ShapeDtypeStruct((t, 256), F32),
        jax.ShapeDtypeStruct((t, 256), BF16),
        jax.ShapeDtypeStruct((t, 256), BF16),
        jax.ShapeDtypeStruct((t, LANES), F32),
    )
    out_specs = (row(512), row(1024), row(1024), pl.BlockSpec((B_HEADS, tm, LANES), lambda i: (0, i, 0)),
                 row(256), row(256), row(256), row(256), row(256), row(LANES))
    return pl.pallas_call(
        _inproj_kernel,
        grid=(t // tm,),
        in_specs=[row(D_MODEL), pl.BlockSpec((1, D_MODEL), lambda i: (0, 0)),
                  pl.BlockSpec((D_MODEL, _C_END), lambda i: (0, 0))],
        out_specs=out_specs,
        out_shape=out_shape,
        compiler_params=_params(("parallel",)),
        name="in_proj",
    )(x, gain.reshape(1, D_MODEL), w)


def _online_update(carry, s, v):
    m, l, acc = carry
    mn = jnp.maximum(m, jnp.max(s, axis=1, keepdims=True))
    c = jnp.exp(m - mn)
    p = jnp.exp(s - mn)
    l = l * c + jnp.sum(p, axis=1, keepdims=True)
    acc = acc * c + jnp.dot(p.astype(BF16), v, preferred_element_type=F32)
    return mn, l, acc


def _diff_prompt_kernel(lam_ref, slopes_ref, q_ref, kv_ref, subln_ref, o_ref, *, tq, lam_init):
    h = pl.program_id(1)
    qi = pl.program_id(2)
    slope = slopes_ref[h]
    lam = lam_ref[0]
    q = q_ref[...]
    lane = lax.broadcasted_iota(I32, (tq, LANES), 1)
    zero = jnp.zeros_like(q)
    qs = (jnp.where(lane < A_DH, q, zero), jnp.where(lane >= A_DH, q, zero))
    rel = (lax.broadcasted_iota(I32, (tq, tq), 0) - lax.broadcasted_iota(I32, (tq, tq), 1))
    srel = slope * rel.astype(F32)

    def tile(kt, carry, diag):
        k0 = pl.multiple_of(kt * tq, tq)
        kk = kv_ref[pl.ds(k0, tq), 0:LANES]
        v = kv_ref[pl.ds(k0, tq), LANES:2 * LANES]
        off = slope * ((qi - kt) * tq).astype(F32)
        out = []
        for c in range(2):
            s = _nt(qs[c], kk) - srel - off
            if diag:
                s = jnp.where(rel >= 0, s, NEG)
            out.append(_online_update(carry[c], s, v))
        return tuple(out)

    init1 = (jnp.full((tq, 1), NEG, F32), jnp.zeros((tq, 1), F32), jnp.zeros((tq, A_VD), F32))
    carry = lax.fori_loop(0, qi, lambda kt, c: tile(kt, c, False), (init1, init1))
    (m1, l1, a1), (m2, l2, a2) = tile(qi, carry, True)
    o = a1 / l1 - lam * (a2 / l2)
    y = o * lax.rsqrt(jnp.mean(o * o, axis=-1, keepdims=True) + NORM_EPS) * subln_ref[...]
    o_ref[...] = (y * (1.0 - lam_init)).astype(BF16)


def _diff_attn_prompt(aq, akvb, lam, subln, b, s, lam_init):
    tq = 256
    assert s % tq == 0
    aq = aq.reshape(b, s, 512)
    akvb = akvb.reshape(b, s, 1024)
    out = pl.pallas_call(
        functools.partial(_diff_prompt_kernel, tq=tq, lam_init=lam_init),
        grid=(b, A_HEADS, s // tq),
        in_specs=[pl.BlockSpec(memory_space=pltpu.SMEM), pl.BlockSpec(memory_space=pltpu.SMEM),
                  pl.BlockSpec((None, tq, LANES), lambda bi, h, qi: (bi, qi, h)),
                  pl.BlockSpec((None, s, 2 * LANES), lambda bi, h, qi: (bi, 0, h)),
                  pl.BlockSpec((1, A_VD), lambda bi, h, qi: (0, 0))],
        out_specs=pl.BlockSpec((None, tq, LANES), lambda bi, h, qi: (bi, qi, h)),
        out_shape=jax.ShapeDtypeStruct((b, s, 512), BF16),
        compiler_params=_params(("parallel", "parallel", "arbitrary")),
        name="diff_attn_prompt",
    )(lam.reshape(1), jnp.asarray(_alibi(A_HEADS)), aq, akvb, subln.reshape(1, A_VD))
    return out.reshape(b * s, 512)


def _prep_compress(cmp_pe, cmp_w1, cmp_w2, v_first=True):
    w1r = cmp_w1.reshape(2, 2, CMP_STRIDE, B_DH, CMP_HID)
    base = w1r.transpose(2, 0, 3, 1, 4)
    wbig = jnp.zeros((CMP_STRIDE, B_KV, 2, B_DH, 2, B_KV, 2, CMP_HID), F32)
    w2big = jnp.zeros((B_KV, 2, CMP_HID, B_KV, 2, B_DH), F32)
    for g in range(B_KV):
        for e in range(2):
            wbig = wbig.at[:, g, e, :, :, g, e, :].set(base[:, e])
            w2big = w2big.at[g, e, :, g, (1 - e) if v_first else e, :].set(cmp_w2[e])
    wbig = wbig.reshape(CMP_STRIDE * 256, 1024).astype(BF16)
    w2big = w2big.reshape(512, 256).astype(BF16)
    per = cmp_pe.reshape(2, 2, CMP_STRIDE, B_DH)
    pec = jnp.broadcast_to(per.transpose(1, 2, 0, 3)[:, :, None], (2, CMP_STRIDE, B_KV, 2, B_DH))
    pec = jnp.pad(pec.reshape(2, CMP_STRIDE * 256), ((0, 6), (0, 0))).astype(BF16)
    return wbig, w2big, pec


def _compress_kernel(*refs, n_in, rows):
    x_refs = refs[-(2 * n_in + 5):-5]
    w1_ref, pe_ref, w2_ref, o_ref, prev_ref = refs[-5:]
    step = pl.program_id(1)
    m_each = rows // CMP_STRIDE

    @pl.when(step == 0)
    def _():
        prev_ref[...] = jnp.zeros_like(prev_ref)

    pieces = []
    for i in range(n_in):
        cols = [x_refs[2 * i + g][pl.ds(s, m_each, stride=CMP_STRIDE), :].astype(BF16)
                for s in range(CMP_STRIDE) for g in range(B_KV)]
        pieces.append(jnp.concatenate(cols, axis=1))
    c = pieces[0] if n_in == 1 else jnp.concatenate(pieces, axis=0)
    m = c.shape[0]
    p = jnp.dot(c, w1_ref[...], preferred_element_type=F32)
    pp = jnp.dot(pe_ref[...], w1_ref[...], preferred_element_type=F32)
    pe_term = pp[0:1, :512] + pp[1:2, 512:]
    p0 = p[:, :512]
    p1 = p[:, 512:]
    row = lax.broadcasted_iota(I32, (m, 512), 0)
    p0s = jnp.where(row == 0, prev_ref[7:8, :], pltpu.roll(p0, 1, 0))
    prev_ref[...] = p0[m - 8:m, :]
    act = _gelu(p0s + p1 + pe_term)
    o_ref[...] = jnp.dot(act.astype(BF16), w2_ref[...], preferred_element_type=F32).astype(BF16)


def _compress_prompt(ckv, b, s, wbig, w2big, pec):
    rows = min(s, 2048)
    assert s % rows == 0 and rows % (8 * CMP_STRIDE) == 0
    n_ch = s // CMP_STRIDE
    m = rows // CMP_STRIDE
    const = lambda shp: pl.BlockSpec(shp, lambda bi, st: (0, 0))
    return pl.pallas_call(
        functools.partial(_compress_kernel, n_in=1, rows=rows),
        grid=(b, s // rows),
        in_specs=[pl.BlockSpec((None, rows, LANES), lambda bi, st: (bi, st, 0)),
                  pl.BlockSpec((None, rows, LANES), lambda bi, st: (bi, st, 1)),
                  const(wbig.shape), const(pec.shape), const(w2big.shape)],
        out_specs=pl.BlockSpec((None, m, 256), lambda bi, st: (bi, st, 0)),
        out_shape=jax.ShapeDtypeStruct((b, n_ch, 256), BF16),
        scratch_shapes=[pltpu.VMEM((8, 512), F32)],
        compiler_params=_params(("parallel", "arbitrary")),
        name="nsa_compress_prompt",
    )(ckv.reshape(b, s, 256), ckv.reshape(b, s, 256), wbig, pec, w2big)


def _rank_rows(score, n_rows):
    jrow = lax.broadcasted_iota(I32, score.shape, 0)
    rank = jnp.zeros(score.shape, I32)
    for j2 in range(n_rows):
        sj = score[j2:j2 + 1, :]
        rank = rank + jnp.where(jrow > j2, (sj >= score).astype(I32), (sj > score).astype(I32))
    return rank


def _nsa_prompt_kernel(slopes_ref, q_ref, ckv_ref, ske_ref, skv_ref, wkv_ref, gates_ref, ovt_ref, onorm_ref,
                       o_ref, *, tq, tk, n_slc):
    g = pl.program_id(1)
    qi = pl.program_id(2)
    q0 = qi * tq
    mrows = B_REP * tq
    n_ch = ckv_ref.shape[0]
    q = q_ref[...].reshape(mrows, LANES)
    rowi = lax.broadcasted_iota(I32, (mrows, 1), 0)
    slope_col = jnp.zeros((mrows, 1), F32)
    for r in range(B_REP):
        slope_col = jnp.where(rowi // tq == r, slopes_ref[g * B_REP + r], slope_col)
    qpos_i = q0 + rowi % tq
    qpos = qpos_i.astype(F32)
    lane = lax.broadcasted_iota(I32, (mrows, LANES), 1)

    ckv = ckv_ref[...]
    s = _nt(q, ckv)
    mcol = lax.broadcasted_iota(I32, (1, n_ch), 1)
    cpos = mcol * CMP_STRIDE + (CMP_STRIDE - 1)
    valid = (mcol >= 1) & (cpos <= qpos_i)
    s = jnp.where(valid, s - slope_col * (qpos - cpos.astype(F32)), NEG)
    p = jnp.exp(s - jnp.max(s, axis=1, keepdims=True))
    p = jnp.where(qpos_i >= CMP_LEN - 1, p / jnp.sum(p, axis=1, keepdims=True), 0.0)
    o_c = jnp.dot(p.astype(BF16), ckv, preferred_element_type=F32)
    psum = p[0:tq] + p[tq:2 * tq] + p[2 * tq:3 * tq] + p[3 * tq:4 * tq]
    hi = psum.astype(BF16)
    lo = (psum - hi.astype(F32)).astype(BF16)
    imp_t = _nt(ovt_ref[...], hi) + _nt(ovt_ref[...], lo)

    jrow = lax.broadcasted_iota(I32, (LANES, tq), 0)
    jcur = (q0 + lax.broadcasted_iota(I32, (LANES, tq), 1)) // SEL_BLOCK
    forced = (jrow == 0) | (jrow == jcur) | (jrow == jcur - 1)
    score = jnp.where(jrow > jcur, -FORCED, jnp.where(forced, FORCED, imp_t))
    rank = _rank_rows(score, n_slc)
    neg = jnp.where(rank < N_SEL, 0.0, NEG).T
    neg4 = jnp.concatenate([neg] * B_REP, axis=0).astype(BF16)
    q_aug = jnp.where(lane < B_DH, neg4, q)

    def attend(carry, qq, kk, vv, k0, n, mask_fn):
        kpos_i = k0 + lax.broadcasted_iota(I32, (1, n), 1)
        sc = _nt(qq, kk) - slope_col * (qpos - kpos_i.astype(F32))
        if mask_fn is not None:
            sc = jnp.where(mask_fn(kpos_i), sc, NEG)
        return _online_update(carry, sc, vv)

    init = (jnp.full((mrows, 1), NEG, F32), jnp.zeros((mrows, 1), F32), jnp.zeros((mrows, LANES), F32))

    def sel_tile(kt, carry, diag):
        k0 = pl.multiple_of(kt * tk, tk)
        return attend(carry, q_aug, ske_ref[pl.ds(k0, tk), :], skv_ref[pl.ds(k0, tk), :], k0, tk,
                      (lambda kp: kp <= qpos_i) if diag else None)

    n_full = q0 // tk
    carry = lax.fori_loop(0, n_full, lambda kt, c: sel_tile(kt, c, False), init)
    _, l_s, a_s = sel_tile(n_full, carry, True)

    carry = init
    for t in range(WINDOW // tq + 1):
        k0 = q0 - WINDOW + t * tq
        k0c = pl.multiple_of(jnp.maximum(k0, 0), tq)
        kv = wkv_ref[pl.ds(k0c, tq), :]

        def wmask(kp, k0=k0):
            d = qpos_i - kp
            return (d >= 0) & (d < WINDOW) & (k0 >= 0)

        carry = attend(carry, q, kv, kv, k0c, tq, wmask)
    _, l_w, a_w = carry

    gates = gates_ref[...]
    glane = lax.broadcasted_iota(I32, (tq, LANES), 1)

    def gate_col(branch):
        cols = [jnp.sum(jnp.where(glane == branch * B_HEADS + g * B_REP + r, gates, 0.0), axis=1, keepdims=True)
                for r in range(B_REP)]
        return jnp.concatenate(cols, axis=0)

    o = gate_col(0) * o_c + gate_col(1) * (a_s / l_s) + gate_col(2) * (a_w / l_w)
    o = jnp.where(lane < B_DH, o, 0.0)
    y = o * lax.rsqrt(jnp.sum(o * o, axis=1, keepdims=True) * (1.0 / B_DH) + NORM_EPS) * onorm_ref[...]
    o_ref[...] = y.reshape(B_REP, tq, LANES)[:, :, :B_DH].astype(BF16)


def _overlap_t(n_rows, n_slc):
    m = np.arange(n_rows)[None, :]
    start = (m - 1) * CMP_STRIDE
    j = np.arange(LANES)[:, None]
    ov = (start < (j + 1) * SEL_BLOCK) & (start + CMP_LEN > j * SEL_BLOCK) & (m >= 1) & (j < n_slc)
    return jnp.asarray(ov.astype(np.float32), BF16)


def _nsa_prompt(bq, ckvc, skvb, wkvb, gates, out_norm, b, s):
    tq, tk = 128, 256
    tk = min(tk, s)
    assert s % tk == 0 and tk % tq == 0 and s >= WINDOW
    n_ch = s // CMP_STRIDE
    n_slc = s // SEL_BLOCK
    assert n_slc <= B_DH
    blk = np.arange(s) // SEL_BLOCK
    onehot = jnp.asarray((blk[:, None] == np.arange(LANES)[None, :]).astype(np.float32), BF16)
    skvb = skvb.reshape(b, s, 256)
    lane = jnp.arange(256) % LANES
    ske = jnp.where(lane < B_DH, jnp.concatenate([onehot, onehot], axis=1)[None], skvb)
    onorm = jnp.pad(out_norm, (0, LANES - B_DH)).reshape(1, LANES)
    grp = lambda n: pl.BlockSpec((None, n, LANES), lambda bi, g, qi: (bi, 0, g))
    out = pl.pallas_call(
        functools.partial(_nsa_prompt_kernel, tq=tq, tk=tk, n_slc=n_slc),
        grid=(b, B_KV, s // tq),
        in_specs=[pl.BlockSpec(memory_space=pltpu.SMEM),
                  pl.BlockSpec((B_REP, None, tq, LANES), lambda bi, g, qi: (g, bi, qi, 0)),
                  grp(n_ch), grp(s), grp(s), grp(s),
                  pl.BlockSpec((None, tq, LANES), lambda bi, g, qi: (bi, qi, 0)),
                  pl.BlockSpec((LANES, n_ch), lambda bi, g, qi: (0, 0)),
                  pl.BlockSpec((1, LANES), lambda bi, g, qi: (0, 0))],
        out_specs=pl.BlockSpec((B_REP, None, tq, B_DH), lambda bi, g, qi: (g, bi, qi, 0)),
        out_shape=jax.ShapeDtypeStruct((B_HEADS, b, s, B_DH), BF16),
        compiler_params=_params(("parallel", "parallel", "arbitrary")),
        name="nsa_prompt",
    )(jnp.asarray(_alibi(B_HEADS)), bq.reshape(B_HEADS, b, s, LANES), ckvc, ske, skvb,
      wkvb.reshape(b, s, 256), gates.reshape(b, s, LANES), _overlap_t(n_ch, n_slc), onorm)
    return out.transpose(1, 2, 0, 3).reshape(b * s, 512)


def _topk_rows(x, k):
    n = x.shape[0]
    rows = lax.broadcasted_iota(I32, x.shape, 0)
    vals, idxs = [], []
    cur = x
    for _ in range(k):
        m = jnp.max(cur, axis=0, keepdims=True)
        idx = jnp.min(jnp.where(cur == m, rows, n), axis=0, keepdims=True)
        vals.append(m)
        idxs.append(idx)
        cur = jnp.where(rows == idx, -jnp.inf, cur)
    return jnp.concatenate(vals, axis=0), jnp.concatenate(idxs, axis=0)


def _route_kernel(x_ref, oa_ref, ob_ref, wo_ref, g_ref, wq_ref, keys_ref, x1_ref, xn_ref, eid_ref, gate_ref):
    x1 = (x_ref[...] + jnp.dot(oa_ref[...], wo_ref[0:512, :], preferred_element_type=F32)
          + jnp.dot(ob_ref[...], wo_ref[512:1024, :], preferred_element_type=F32))
    x1_ref[...] = x1
    xn = x1 * lax.rsqrt(jnp.mean(x1 * x1, axis=-1, keepdims=True) + NORM_EPS) * g_ref[...]
    xn_ref[...] = xn
    q = jnp.dot(xn.astype(BF16), wq_ref[...], preferred_element_type=F32).astype(BF16)
    for h in range(PEER_HEADS):
        hv, hi = [], []
        for p in range(2):
            c0 = (h * 2 + p) * LANES
            sc = _nt(keys_ref[p], q[:, c0:c0 + LANES])
            v, i = _topk_rows(sc, PEER_TOPK)
            hv.append(v)
            hi.append(i)
        comb = jnp.concatenate([hv[0][a:a + 1] + hv[1] for a in range(PEER_TOPK)], axis=0)
        top, ci = _topk_rows(comb, PEER_TOPK)
        ca = ci // PEER_TOPK
        cb = ci % PEER_TOPK
        i1 = jnp.zeros_like(ci)
        i2 = jnp.zeros_like(ci)
        for a in range(PEER_TOPK):
            i1 = jnp.where(ca == a, hi[0][a:a + 1], i1)
            i2 = jnp.where(cb == a, hi[1][a:a + 1], i2)
        e = jnp.exp(top - jnp.max(top, axis=0, keepdims=True))
        eid_ref[h * PEER_TOPK:(h + 1) * PEER_TOPK, :] = i1 * PEER_NKEYS + i2
        gate_ref[h * PEER_TOPK:(h + 1) * PEER_TOPK, :] = e / jnp.sum(e, axis=0, keepdims=True)


def _route(x, oa, ob, w_out, norm_ffn, peer_wq, peer_subkeys):
    t = x.shape[0]
    tm = 256
    assert t % tm == 0
    nt = t // tm
    row = lambda n: pl.BlockSpec((tm, n), lambda i: (i, 0))
    const = lambda shp: pl.BlockSpec(shp, lambda i: (0,) * len(shp))
    tr = pl.BlockSpec((None, PEER_HEADS * PEER_TOPK, tm), lambda i: (i, 0, 0))
    x1, xn, eid_t, gate_t = pl.pallas_call(
        _route_kernel,
        grid=(nt,),
        in_specs=[row(D_MODEL), row(512), row(512), const((D_MODEL, D_MODEL)), const((1, D_MODEL)),
                  const((D_MODEL, PEER_HEADS * PEER_DK)), const((2, PEER_NKEYS, PEER_DK // 2))],
        out_specs=(row(D_MODEL), row(D_MODEL), tr, tr),
        out_shape=(jax.ShapeDtypeStruct((t, D_MODEL), F32), jax.ShapeDtypeStruct((t, D_MODEL), F32),
                   jax.ShapeDtypeStruct((nt, 128, tm), I32), jax.ShapeDtypeStruct((nt, 128, tm), F32)),
        compiler_params=_params(("parallel",)),
        name="out_proj_route",
    )(x, oa, ob, w_out.astype(BF16), norm_ffn.reshape(1, D_MODEL), peer_wq.astype(BF16), peer_subkeys.astype(BF16))
    eid = eid_t.transpose(0, 2, 1).reshape(t, 128)
    gate = gate_t.transpose(0, 2, 1).reshape(t, 128)
    return x1, xn, eid, gate


_N_PAIRS = PEER_HEADS * PEER_TOPK
_TILE_ROWS = 16


_KROWS = _N_PAIRS * _TILE_ROWS


def _gather_pair(tab_ref, row_ref, ta, tb_):
    ga = jnp.concatenate([tab_ref[row_ref[ta, j]] for j in range(_N_PAIRS)], axis=0)
    gb = jnp.concatenate([tab_ref[row_ref[tb_, j]] for j in range(_N_PAIRS)], axis=0)
    return jnp.concatenate([ga, gb], axis=1)


def _diag_mask():
    return (lax.broadcasted_iota(I32, (8, _KROWS), 1) % 8) == lax.broadcasted_iota(I32, (8, _KROWS), 0)


def _peer_hidden_kernel(row_ref, x_ref, par_ref, gate_ref, sum_ref, u_ref, o_ref, d_ref, *, tb):
    diag = _diag_mask()
    zero = jnp.zeros((8, LANES), BF16)

    def pair(i, carry):
        ta = 2 * i
        xa = x_ref[ta].astype(BF16)
        xb = x_ref[ta + 1].astype(BF16)
        lhs = jnp.concatenate([jnp.concatenate([xa, zero], axis=1), jnp.concatenate([zero, xb], axis=1)], axis=0)
        res = _nt(lhs, _gather_pair(u_ref, row_ref, ta, ta + 1))
        d_ref[pl.ds(ta, 1), :] = jnp.sum(jnp.where(diag, res[0:8], 0.0), axis=0, keepdims=True)
        d_ref[pl.ds(ta + 1, 1), :] = jnp.sum(jnp.where(diag, res[8:16], 0.0), axis=0, keepdims=True)
        return carry

    lax.fori_loop(0, tb // 2, pair, 0, unroll=4)
    d = d_ref[...]
    hi = d.astype(BF16)
    lo = (d - hi.astype(F32)).astype(BF16)
    hl = (jnp.dot(hi, sum_ref[...], preferred_element_type=F32)
          + jnp.dot(lo, sum_ref[...], preferred_element_type=F32))
    hdn = jnp.where(par_ref[...] == 1, hl[:, _N_PAIRS:], hl[:, :_N_PAIRS])
    o_ref[...] = gate_ref[...] * _gelu(hdn)


def _peer_out_kernel(row_ref, par_ref, w_ref, x1_ref, gain_ref, exlo_ref, exhi_ref, v_ref, o_ref, wl_ref, *, tb):
    par = par_ref[...].astype(F32)
    w = w_ref[...]
    wl_ref[...] = (jnp.dot((w * (1.0 - par)).astype(BF16), exlo_ref[...], preferred_element_type=F32)
                   + jnp.dot((w * par).astype(BF16), exhi_ref[...], preferred_element_type=F32))
    diag = _diag_mask()

    def finish(t, acc):
        x2 = x1_ref[t] + acc
        ss = jnp.sum(jnp.sum(x2 * x2, axis=1, keepdims=True), axis=0, keepdims=True)
        o_ref[t] = x2 * lax.rsqrt(ss * (1.0 / D_MODEL) + NORM_EPS) * gain_ref[...]

    def pair(i, carry):
        ta = 2 * i
        lhs = jnp.concatenate(
            [jnp.where(diag, jnp.broadcast_to(wl_ref[pl.ds(t, 1), :], (8, _KROWS)), 0.0) for t in (ta, ta + 1)],
            axis=0).astype(BF16)
        res = jnp.dot(lhs, _gather_pair(v_ref, row_ref, ta, ta + 1), preferred_element_type=F32)
        finish(ta, res[0:8, 0:LANES])
        finish(ta + 1, res[8:16, LANES:2 * LANES])
        return carry

    lax.fori_loop(0, tb // 2, pair, 0, unroll=4)


def _peer_consts():
    col = np.arange(_KROWS)
    own = (col[None, :] // _TILE_ROWS) == np.arange(_N_PAIRS)[:, None]
    half = (col[None, :] // 8) % 2
    exlo = (own & (half == 0)).astype(np.float32)
    exhi = (own & (half == 1)).astype(np.float32)
    summat = np.concatenate([exlo.T, exhi.T], axis=1)
    return jnp.asarray(exlo, BF16), jnp.asarray(exhi, BF16), jnp.asarray(summat, BF16)


def _peer_experts(x1, xn, eid, gate, u_tab, v_tab, norm_final):
    t = x1.shape[0]
    tb = 128
    assert t % tb == 0
    smem = pl.BlockSpec((tb, _N_PAIRS), lambda i: (i, 0), memory_space=pltpu.SMEM)
    vrow = pl.BlockSpec((tb, _N_PAIRS), lambda i: (i, 0))
    tok = pl.BlockSpec((tb, 8, LANES), lambda i: (i, 0, 0))
    table = pl.BlockSpec(memory_space=pltpu.VMEM)
    const = lambda shp: pl.BlockSpec(shp, lambda i: (0,) * len(shp))
    row = eid >> 1
    par = eid & 1
    exlo, exhi, summat = _peer_consts()
    w = pl.pallas_call(
        functools.partial(_peer_hidden_kernel, tb=tb),
        grid=(t // tb,),
        in_specs=[smem, tok, vrow, vrow, const(summat.shape), table],
        out_specs=vrow,
        out_shape=jax.ShapeDtypeStruct((t, _N_PAIRS), F32),
        scratch_shapes=[pltpu.VMEM((tb, _KROWS), F32)],
        compiler_params=_params(("arbitrary",)),
        name="peer_hidden",
    )(row, xn.reshape(t, 8, LANES), par, gate, summat, u_tab)
    y = pl.pallas_call(
        functools.partial(_peer_out_kernel, tb=tb),
        grid=(t // tb,),
        in_specs=[smem, vrow, vrow, tok, const((8, LANES)), const(exlo.shape), const(exhi.shape), table],
        out_specs=tok,
        out_shape=jax.ShapeDtypeStruct((t, 8, LANES), F32),
        scratch_shapes=[pltpu.VMEM((tb, _KROWS), F32)],
        compiler_params=_params(("arbitrary",)),
        name="peer_out",
    )(row, par, w, x1.reshape(t, 8, LANES), norm_final.reshape(8, LANES), exlo, exhi, v_tab)
    return y.reshape(t, D_MODEL)


_PAGES_PER_STEP = 8


def _head_rows(n_rows, per_head, values_ref, base):
    rowi = lax.broadcasted_iota(I32, (n_rows, 1), 0)
    col = jnp.zeros((n_rows, 1), F32)
    for h in range(n_rows // per_head):
        col = jnp.where(rowi // per_head == h, values_ref[base + h], col)
    return col


def _diff_sample_kernel(pt_ref, lam_ref, slopes_ref, *refs, nq, past, page, lam_init):
    del pt_ref
    pages = refs[:_PAGES_PER_STEP]
    wq_ref, new_ref, subln_ref, o_ref, m_ref, l_ref, acc_ref = refs[_PAGES_PER_STEP:]
    st = pl.program_id(1)
    n_rows = A_HEADS * 2 * nq
    slope_col = _head_rows(n_rows, 2 * nq, slopes_ref, 0)
    rowi = lax.broadcasted_iota(I32, (n_rows, 1), 0)
    qpos_i = past + rowi % nq
    qpos = qpos_i.astype(F32)
    wq = wq_ref[...]

    @pl.when(st == 0)
    def _():
        m_ref[...] = jnp.full_like(m_ref, NEG)
        l_ref[...] = jnp.zeros_like(l_ref)
        acc_ref[...] = jnp.zeros_like(acc_ref)

    def update(rows_bf, k0, n, mask):
        kpos_i = k0 + lax.broadcasted_iota(I32, (1, n), 1)
        s = _nt(wq, rows_bf) - slope_col * (qpos - kpos_i.astype(F32))
        if mask is not None:
            s = jnp.where(mask(kpos_i), s, NEG)
        v = jnp.concatenate([rows_bf[:, h * 256 + 128:(h + 1) * 256] for h in range(A_HEADS)], axis=1)
        m, l, acc = _online_update((m_ref[...], l_ref[...], acc_ref[...]), s, v)
        m_ref[...] = m
        l_ref[...] = l
        acc_ref[...] = acc

    rows = jnp.concatenate([pages[i][...].astype(BF16) for i in range(_PAGES_PER_STEP)], axis=0)
    update(rows, st * _PAGES_PER_STEP * page, _PAGES_PER_STEP * page, None)

    @pl.when(st == pl.num_programs(1) - 1)
    def _():
        new = jnp.concatenate([new_ref[...], jnp.zeros((LANES - nq, new_ref.shape[1]), BF16)], axis=0)
        update(new, past, LANES, lambda kp: kp <= qpos_i)
        o = acc_ref[...] / l_ref[...]
        lam = lam_ref[0]
        for h in range(A_HEADS):
            blk = o[h * 2 * nq:(h + 1) * 2 * nq, h * A_VD:(h + 1) * A_VD]
            oh = blk[0:nq] - lam * blk[nq:2 * nq]
            y = oh * lax.rsqrt(jnp.mean(oh * oh, axis=-1, keepdims=True) + NORM_EPS) * subln_ref[...]
            o_ref[:, h * A_VD:(h + 1) * A_VD] = (y * (1.0 - lam_init)).astype(BF16)


def _diff_attn_sample(aq, akvb, cache_a, page_table, lam, subln, db, nq, lam_init):
    n_phys, page = cache_a.shape[1], cache_a.shape[2]
    n_pages = page_table.shape[1]
    past = n_pages * page
    assert n_pages % _PAGES_PER_STEP == 0 and nq == 8
    n_rows = A_HEADS * 2 * nq
    q5 = aq.reshape(db, nq, A_HEADS, 2, A_DH).transpose(0, 2, 3, 1, 4)
    wq = jnp.einsum('bhcqd,hH,cC->bhcqHCd', q5, jnp.eye(A_HEADS, dtype=BF16), jnp.eye(2, dtype=BF16))
    wq = jnp.pad(wq.reshape(db, n_rows, A_HEADS, 2 * A_DH), ((0, 0), (0, 0), (0, 0), (0, A_VD)))
    wq = wq.reshape(db, n_rows, 1024)
    cache = cache_a.reshape(n_phys, page, 1024)
    page_spec = lambda i: pl.BlockSpec((None, page, 1024),
                                       lambda b, st, pt: (pt[b, st * _PAGES_PER_STEP + i], 0, 0))
    per_b = lambda r, c: pl.BlockSpec((None, r, c), lambda b, st, pt: (b, 0, 0))
    grid_spec = pltpu.PrefetchScalarGridSpec(
        num_scalar_prefetch=1,
        grid=(db, n_pages // _PAGES_PER_STEP),
        in_specs=[pl.BlockSpec(memory_space=pltpu.SMEM), pl.BlockSpec(memory_space=pltpu.SMEM)]
        + [page_spec(i) for i in range(_PAGES_PER_STEP)]
        + [per_b(n_rows, 1024), per_b(nq, 1024), pl.BlockSpec((1, A_VD), lambda b, st, pt: (0, 0))],
        out_specs=per_b(nq, 512),
        scratch_shapes=[pltpu.VMEM((n_rows, 1), F32), pltpu.VMEM((n_rows, 1), F32), pltpu.VMEM((n_rows, 512), F32)],
    )
    out = pl.pallas_call(
        functools.partial(_diff_sample_kernel, nq=nq, past=past, page=page, lam_init=lam_init),
        grid_spec=grid_spec,
        out_shape=jax.ShapeDtypeStruct((db, nq, 512), BF16),
        compiler_params=_params(("parallel", "arbitrary")),
        name="diff_attn_sample",
    )(page_table, lam.reshape(1), jnp.asarray(_alibi(A_HEADS)), *([cache] * _PAGES_PER_STEP), wq,
      akvb.reshape(db, nq, 1024), subln.reshape(1, A_VD))
    return out.reshape(db * nq, 512)


def _compress_sample(cache_cmp, page_table, wbig, w2big, pec, db):
    n_phys, page = cache_cmp.shape[1], cache_cmp.shape[2]
    n_pages = page_table.shape[1]
    assert n_pages % _PAGES_PER_STEP == 0 and page % (8 * CMP_STRIDE) == 0
    m = _PAGES_PER_STEP * page // CMP_STRIDE
    n_ch = n_pages * page // CMP_STRIDE
    cache = cache_cmp.reshape(n_phys, page, 256)
    specs = []
    for i in range(_PAGES_PER_STEP):
        for g in range(B_KV):
            specs.append(pl.BlockSpec((None, page, LANES),
                                      lambda b, st, pt, i=i, g=g: (pt[b, st * _PAGES_PER_STEP + i], 0, g)))
    const = lambda shp: pl.BlockSpec(shp, lambda b, st, pt: (0, 0))
    grid_spec = pltpu.PrefetchScalarGridSpec(
        num_scalar_prefetch=1,
        grid=(db, n_pages // _PAGES_PER_STEP),
        in_specs=specs + [const(wbig.shape), const(pec.shape), const(w2big.shape)],
        out_specs=pl.BlockSpec((None, m, 256), lambda b, st, pt: (b, st, 0)),
        scratch_shapes=[pltpu.VMEM((8, 512), F32)],
    )
    return pl.pallas_call(
        functools.partial(_compress_kernel, n_in=_PAGES_PER_STEP, rows=page),
        grid_spec=grid_spec,
        out_shape=jax.ShapeDtypeStruct((db, n_ch, 256), BF16),
        compiler_params=_params(("parallel", "arbitrary")),
        name="nsa_compress_sample",
    )(page_table, *([cache] * (2 * _PAGES_PER_STEP)), wbig, pec, w2big)


def _nsa_sample_kernel(pt_ref, slopes_ref, *refs, nq, past, page, n_slc):
    del pt_ref
    pages = refs[:_PAGES_PER_STEP]
    (wq_ref, ckv_ref, win_ref, snew_ref, wnew_ref, gates_ref, ovt_ref, onorm_ref, o_ref,
     neg_ref, oc_ref, m_ref, l_ref, acc_ref) = refs[_PAGES_PER_STEP:]
    st = pl.program_id(1)
    n_rows = B_HEADS * nq
    n_j = neg_ref.shape[1]
    slope_col = _head_rows(n_rows, nq, slopes_ref, 0)
    rowi = lax.broadcasted_iota(I32, (n_rows, 1), 0)
    qpos_i = past + rowi % nq
    qpos = qpos_i.astype(F32)
    wq = wq_ref[...]
    init = (jnp.full((n_rows, 1), NEG, F32), jnp.zeros((n_rows, 1), F32), jnp.zeros((n_rows, 256), F32))

    def pad_rows(x):
        return jnp.concatenate([x, jnp.zeros((LANES - x.shape[0], x.shape[1]), x.dtype)], axis=0)

    def scores(rows_bf, kpos_i):
        return _nt(wq, rows_bf) - slope_col * (qpos - kpos_i.astype(F32))

    def block_bias(blk_of_key):
        n = blk_of_key.shape[1]
        onehot = (lax.broadcasted_iota(I32, (n_j, n), 0) == blk_of_key).astype(BF16)
        return jnp.dot(neg_ref[...], onehot, preferred_element_type=F32)

    @pl.when(st == 0)
    def _():
        ckv = ckv_ref[...]
        n_ch = ckv.shape[0]
        mcol = lax.broadcasted_iota(I32, (1, n_ch), 1)
        cpos = mcol * CMP_STRIDE + (CMP_STRIDE - 1)
        valid = (mcol >= 1) & (cpos <= qpos_i)
        s = jnp.where(valid, scores(ckv, cpos), NEG)
        p = jnp.exp(s - jnp.max(s, axis=1, keepdims=True))
        p = jnp.where(qpos_i >= CMP_LEN - 1, p / jnp.sum(p, axis=1, keepdims=True), 0.0)
        oc_ref[...] = jnp.dot(p.astype(BF16), ckv, preferred_element_type=F32)
        grp = B_REP * nq
        psum = jnp.concatenate(
            [sum(p[g * grp + r * nq:g * grp + (r + 1) * nq] for r in range(B_REP)) for g in range(B_KV)], axis=0)
        hi = psum.astype(BF16)
        lo = (psum - hi.astype(F32)).astype(BF16)
        imp_t = _nt(ovt_ref[...], hi) + _nt(ovt_ref[...], lo)
        shp = imp_t.shape
        jrow = lax.broadcasted_iota(I32, shp, 0)
        jcur = (past + lax.broadcasted_iota(I32, shp, 1) % nq) // SEL_BLOCK
        forced = (jrow == 0) | (jrow == jcur) | (jrow == jcur - 1)
        score = jnp.where(jrow > jcur, -FORCED, jnp.where(forced, FORCED, imp_t))
        rank = _rank_rows(score, n_slc)
        neg_t = jnp.where(rank < N_SEL, 0.0, NEG)
        neg = jnp.concatenate([neg_t, jnp.zeros((n_j, LANES - shp[1]), F32)], axis=1).T
        neg = jnp.concatenate([neg[g * nq:(g + 1) * nq] for g in range(B_KV) for _ in range(B_REP)], axis=0)
        neg_ref[...] = neg.astype(BF16)
        m_ref[...], l_ref[...], acc_ref[...] = init

    rows = jnp.concatenate([pages[i][...].astype(BF16) for i in range(_PAGES_PER_STEP)], axis=0)
    n_keys = _PAGES_PER_STEP * page
    kpos_i = st * n_keys + lax.broadcasted_iota(I32, (1, n_keys), 1)
    m_ref[...], l_ref[...], acc_ref[...] = _online_update(
        (m_ref[...], l_ref[...], acc_ref[...]), scores(rows, kpos_i) + block_bias(kpos_i // SEL_BLOCK), rows)

    @pl.when(st == pl.num_programs(1) - 1)
    def _():
        npos_i = past + lax.broadcasted_iota(I32, (1, LANES), 1)
        causal = npos_i <= qpos_i
        snew = pad_rows(snew_ref[...].astype(BF16))
        s = jnp.where(causal, scores(snew, npos_i) + block_bias(npos_i // SEL_BLOCK), NEG)
        _, l_s, a_s = _online_update((m_ref[...], l_ref[...], acc_ref[...]), s, snew)
        win = win_ref[...].astype(BF16)
        wb = win.shape[0]
        wpos_i = past - wb + lax.broadcasted_iota(I32, (1, wb), 1)
        d = qpos_i - wpos_i
        cw = _online_update(init, jnp.where((d >= 0) & (d < WINDOW), scores(win, wpos_i), NEG), win)
        wnew = pad_rows(wnew_ref[...].astype(BF16))
        _, l_w, a_w = _online_update(cw, jnp.where(causal, scores(wnew, npos_i), NEG), wnew)
        gates = jnp.concatenate([gates_ref[...]] * B_HEADS, axis=0)
        glane = lax.broadcasted_iota(I32, (n_rows, LANES), 1)

        def gate_col(branch):
            return jnp.sum(jnp.where(glane == branch * B_HEADS + rowi // nq, gates, 0.0), axis=1, keepdims=True)

        o = gate_col(0) * oc_ref[...] + gate_col(1) * (a_s / l_s) + gate_col(2) * (a_w / l_w)
        lane = lax.broadcasted_iota(I32, (nq, LANES), 1)
        outs = []
        for hd in range(B_HEADS):
            g = hd // B_REP
            blk = jnp.where(lane >= B_DH, o[hd * nq:(hd + 1) * nq, g * LANES:(g + 1) * LANES], 0.0)
            y = blk * lax.rsqrt(jnp.sum(blk * blk, axis=1, keepdims=True) * (1.0 / B_DH) + NORM_EPS) * onorm_ref[...]
            outs.append(y[:, B_DH:])
        o_ref[...] = jnp.concatenate(outs, axis=1).astype(BF16)


def _overlap_rows(n_rows, n_slc, n_j):
    m = np.arange(n_rows)[None, :]
    start = (m - 1) * CMP_STRIDE
    j = np.arange(n_j)[:, None]
    ov = (start < (j + 1) * SEL_BLOCK) & (start + CMP_LEN > j * SEL_BLOCK) & (m >= 1) & (j < n_slc)
    return jnp.asarray(ov.astype(np.float32), BF16)


def _nsa_sample(bq, ckv, skv, wkv, gates, cache_cmp, cache_sel, state_win, page_table, cmp_pe, cmp_w1, cmp_w2,
                out_norm, db, nq):
    n_phys, page = cache_sel.shape[1], cache_sel.shape[2]
    n_pages = page_table.shape[1]
    past = n_pages * page
    wb = state_win.shape[2]
    assert nq == 8 and nq < CMP_STRIDE and past % SEL_BLOCK == 0 and wb == WINDOW and page % SEL_BLOCK == 0
    n_ch = past // CMP_STRIDE
    n_slc = -(-(past + nq) // SEL_BLOCK)
    n_j = -(-n_slc // LANES) * LANES
    n_rows = B_HEADS * nq
    wbig, w2big, pec = _prep_compress(cmp_pe, cmp_w1, cmp_w2, v_first=False)
    ckvc = _compress_sample(cache_cmp, page_table, wbig, w2big, pec, db)
    q = bq[:, :, B_DH:].reshape(B_KV, B_REP, db, nq, B_DH)
    wq = jnp.einsum('grbqd,gG->bgrqGd', q, jnp.eye(B_KV, dtype=BF16))
    wq = jnp.pad(wq, ((0, 0),) * 5 + ((0, B_DH),)).reshape(db, n_rows, 256)
    onorm = jnp.pad(out_norm, (B_DH, 0)).reshape(1, LANES)
    cache = cache_sel.reshape(n_phys, page, 256)
    page_spec = lambda i: pl.BlockSpec((None, page, 256), lambda b, st, pt: (pt[b, st * _PAGES_PER_STEP + i], 0, 0))
    per_b = lambda r, c: pl.BlockSpec((None, r, c), lambda b, st, pt: (b, 0, 0))
    const = lambda shp: pl.BlockSpec(shp, lambda b, st, pt: (0, 0))
    grid_spec = pltpu.PrefetchScalarGridSpec(
        num_scalar_prefetch=1,
        grid=(db, n_pages // _PAGES_PER_STEP),
        in_specs=[pl.BlockSpec(memory_space=pltpu.SMEM)] + [page_spec(i) for i in range(_PAGES_PER_STEP)]
        + [per_b(n_rows, 256), per_b(n_ch, 256), per_b(wb, 256), per_b(nq, 256), per_b(nq, 256), per_b(nq, LANES),
           const((n_j, n_ch)), const((1, LANES))],
        out_specs=per_b(nq, 512),
        scratch_shapes=[pltpu.VMEM((n_rows, n_j), BF16), pltpu.VMEM((n_rows, 256), F32), pltpu.VMEM((n_rows, 1), F32),
                        pltpu.VMEM((n_rows, 1), F32), pltpu.VMEM((n_rows, 256), F32)],
    )
    out = pl.pallas_call(
        functools.partial(_nsa_sample_kernel, nq=nq, past=past, page=page, n_slc=n_slc),
        grid_spec=grid_spec,
        out_shape=jax.ShapeDtypeStruct((db, nq, 512), BF16),
        compiler_params=_params(("parallel", "arbitrary")),
        name="nsa_sample",
    )(page_table, jnp.asarray(_alibi(B_HEADS)), *([cache] * _PAGES_PER_STEP), wq, ckvc,
      state_win.reshape(db, wb, 256), skv.reshape(db, nq, 256), wkv.reshape(db, nq, 256), gates.reshape(db, nq, LANES),
      _overlap_rows(n_ch, n_slc, n_j), onorm)
    return out.reshape(db * nq, 512)


def kernel(x_prompt, x_sample, cache_a, cache_cmp, cache_sel, state_win, page_table, norm_mix, w_in, diff_lambda,
           diff_subln, cmp_pe, cmp_w1, cmp_w2, nsa_out_norm, w_out, norm_ffn, peer_wq, peer_subkeys, peer_u, peer_v,
           norm_final):
    b, s, _ = x_prompt.shape
    db, nq, _ = x_sample.shape
    lam_init = 0.8 - 0.6 * math.exp(-0.3 * 0)
    dl = diff_lambda[0]
    lam = jnp.exp(jnp.sum(dl[0] * dl[1])) - jnp.exp(jnp.sum(dl[2] * dl[3])) + lam_init
    w = _prep_w_in(w_in[0])
    wbig, w2big, pec = _prep_compress(cmp_pe[0], cmp_w1[0], cmp_w2[0])

    xp = x_prompt.reshape(b * s, D_MODEL)
    aq, akv, akvb, bq, ckv, skv, wkv, skvb, wkvb, gates = _in_proj(xp, norm_mix[0], w)
    oa = _diff_attn_prompt(aq, akvb, lam, diff_subln[0], b, s, lam_init)
    ckvc = _compress_prompt(ckv, b, s, wbig, w2big, pec)
    ob = _nsa_prompt(bq, ckvc, skvb, wkvb, gates, nsa_out_norm[0], b, s)
    x1, xn, eid, gate = _route(xp, oa, ob, w_out[0], norm_ffn[0], peer_wq[0], peer_subkeys[0])
    u_tab = peer_u[0].astype(BF16).reshape(-1, 16, LANES)
    v_tab = peer_v[0].astype(BF16).reshape(-1, 16, LANES)
    y_prompt = _peer_experts(x1, xn, eid, gate, u_tab, v_tab, norm_final).reshape(b, s, D_MODEL)
    win = min(WINDOW, s)
    outs_p = (akv.reshape(1, b, s, A_HEADS, 256), ckv.reshape(1, b, s, B_KV, 128), skv.reshape(1, b, s, B_KV, 128),
              wkv.reshape(b, s, B_KV, 128)[None, :, s - win:])

    ts = db * nq
    xs = x_sample.reshape(ts, D_MODEL)
    saq, sakv, sakvb, sbq, sckv, sskv, swkv, _, _, sgates = _in_proj(xs, norm_mix[0], w)
    soa = _diff_attn_sample(saq, sakvb, cache_a, page_table, lam, diff_subln[0], db, nq, lam_init)
    sob = _nsa_sample(sbq, sckv, sskv, swkv, sgates, cache_cmp, cache_sel, state_win, page_table, cmp_pe[0],
                      cmp_w1[0], cmp_w2[0], nsa_out_norm[0], db, nq)
    sx1, sxn, seid, sgate = _route(xs, soa, sob, w_out[0], norm_ffn[0], peer_wq[0], peer_subkeys[0])
    y_sample = _peer_experts(sx1, sxn, seid, sgate, u_tab, v_tab, norm_final).reshape(db, nq, D_MODEL)
    new_win = jnp.concatenate([state_win[0][:, nq:], swkv.reshape(db, nq, B_KV, 128)], axis=1)
    return (y_prompt, y_sample, outs_p[0], sakv.reshape(1, db, nq, A_HEADS, 256), outs_p[1],
            sckv.reshape(1, db, nq, B_KV, 128), outs_p[2], sskv.reshape(1, db, nq, B_KV, 128), outs_p[3],
            new_win[None])
```

```python
import functools
import math

import jax
import jax.numpy as jnp
import numpy as np
from jax import lax
from jax.experimental import pallas as pl
from jax.experimental.pallas import tpu as pltpu

F32 = jnp.float32
BF16 = jnp.bfloat16
I32 = jnp.int32

D_MODEL = 1024
A_HEADS = 4
A_DH = 64
A_VD = 128
B_HEADS = 8
B_DH = 64
B_KV = 2
B_REP = 4
CMP_STRIDE = 16
CMP_LEN = 32
CMP_HID = 128
SEL_BLOCK = 64
N_SEL = 16
WINDOW = 512
PEER_HEADS = 8
PEER_NKEYS = 128
PEER_DK = 256
PEER_TOPK = 16
NORM_EPS = 1e-6
NEG = -1e30
FORCED = 1e9

VMEM_LIMIT_V7X = 52 * 1024 * 1024
LANES = 128

_C_AQ = 0
_C_AKV = 512
_C_BQ = 1536
_C_CKV = 2560
_C_SKV = 2816
_C_WKV = 3072
_C_GATE = 3328
_C_END = 3456


def _nt(a, b):
    return lax.dot_general(a, b, (((1,), (1,)), ((), ())), preferred_element_type=F32)


def _gelu(x):
    return 0.5 * x * (1.0 + lax.erf(x * 0.7071067811865476))


def _params(sem):
    return pltpu.CompilerParams(dimension_semantics=sem, vmem_limit_bytes=VMEM_LIMIT_V7X)


def _alibi(n):
    return np.asarray(2.0 ** (-8.0 * np.arange(1, n + 1) / n), dtype=np.float32)


def _prep_w_in(w_in):
    d = w_in.shape[0]
    aq = w_in[:, 0:512] * 0.125
    ak = w_in[:, 512:1024].reshape(d, A_HEADS, 2 * A_DH)
    av = w_in[:, 1024:1536].reshape(d, A_HEADS, A_VD)
    akv = jnp.concatenate([ak, av], axis=-1).reshape(d, 1024)
    bq = w_in[:, 1536:2048].reshape(d, B_HEADS, B_DH) * 0.125
    bq = jnp.concatenate([jnp.zeros_like(bq), bq], axis=-1).reshape(d, 1024)
    ckv = w_in[:, 2048:2304]
    skv = w_in[:, 2304:2560]
    wkv = w_in[:, 2560:2816]
    gt = w_in[:, 2816:2840].reshape(d, B_HEADS, 3).transpose(0, 2, 1).reshape(d, 24)
    gt = jnp.pad(gt, ((0, 0), (0, LANES - 24)))
    return jnp.concatenate([aq, akv, bq, ckv, skv, wkv, gt], axis=1).astype(BF16)


def _inproj_kernel(x_ref, g_ref, w_ref, aq_ref, akv_ref, akvb_ref, bq_ref, ckv_ref, skv_ref, wkv_ref,
                   skvb_ref, wkvb_ref, gate_ref):
    x = x_ref[...]
    xn = x * lax.rsqrt(jnp.mean(x * x, axis=-1, keepdims=True) + NORM_EPS) * g_ref[...]
    xb = xn.astype(BF16)

    def seg(a, b):
        return jnp.dot(xb, w_ref[:, a:b], preferred_element_type=F32)

    aq_ref[...] = seg(_C_AQ, _C_AKV).astype(BF16)
    akv = seg(_C_AKV, _C_BQ)
    akv_ref[...] = akv
    akvb_ref[...] = akv.astype(BF16)
    bq = seg(_C_BQ, _C_CKV).astype(BF16)
    for hd in range(B_HEADS):
        bq_ref[hd] = bq[:, hd * LANES:(hd + 1) * LANES]
    ckv_ref[...] = seg(_C_CKV, _C_SKV)
    for (a, fref, bref) in ((_C_SKV, skv_ref, skvb_ref), (_C_WKV, wkv_ref, wkvb_ref)):
        kv = seg(a, a + 2 * LANES)
        fref[...] = kv
        sw = jnp.concatenate([pltpu.roll(kv[:, g * LANES:(g + 1) * LANES], 64, 1) for g in range(B_KV)], axis=1)
        bref[...] = sw.astype(BF16)
    gate_ref[...] = jax.nn.sigmoid(seg(_C_GATE, _C_END))


def _in_proj(x, gain, w):
    t = x.shape[0]
    tm = 256
    assert t % tm == 0
    row = lambda n: pl.BlockSpec((tm, n), lambda i: (i, 0))
    out_shape = (
        jax.ShapeDtypeStruct((t, 512), BF16),
        jax.ShapeDtypeStruct((t, 1024), F32),
        jax.ShapeDtypeStruct((t, 1024), BF16),
        jax.ShapeDtypeStruct((B_HEADS, t, LANES), BF16),
        jax.ShapeDtypeStruct((t, 256), F32),
        jax.ShapeDtypeStruct((t, 256), F32),
        jax.ShapeDtypeStruct((t, 256), F32),
        jax.ShapeDtypeStruct((t, 256), BF16),
        jax.ShapeDtypeStruct((t, 256), BF16),
        jax.ShapeDtypeStruct((t, LANES), F32),
    )
    out_specs = (row(512), row(1024), row(1024), pl.BlockSpec((B_HEADS, tm, LANES), lambda i: (0, i, 0)),
                 row(256), row(256), row(256), row(256), row(256), row(LANES))
    return pl.pallas_call(
        _inproj_kernel,
        grid=(t // tm,),
        in_specs=[row(D_MODEL), pl.BlockSpec((1, D_MODEL), lambda i: (0, 0)),
                  pl.BlockSpec((D_MODEL, _C_END), lambda i: (0, 0))],
        out_specs=out_specs,
        out_shape=out_shape,
        compiler_params=_params(("parallel",)),
        name="in_proj",
    )(x, gain.reshape(1, D_MODEL), w)


def _online_update(carry, s, v):
    m, l, acc = carry
    mn = jnp.maximum(m, jnp.max(s, axis=1, keepdims=True))
    c = jnp.exp(m - mn)
    p = jnp.exp(s - mn)
    l = l * c + jnp.sum(p, axis=1, keepdims=True)
    acc = acc * c + jnp.dot(p.astype(BF16), v, preferred_element_type=F32)
    return mn, l, acc


def _diff_prompt_kernel(lam_ref, slopes_ref, q_ref, kv_ref, subln_ref, o_ref, *, tq, tk, lam_init):
    qi = pl.program_id(1)
    q0 = qi * tq
    lam = lam_ref[0]
    lane = lax.broadcasted_iota(I32, (tq, LANES), 1)
    rel = (lax.broadcasted_iota(I32, (tq, tk), 0) - lax.broadcasted_iota(I32, (tq, tk), 1))
    relf = rel.astype(F32)
    qs = []
    for h in range(A_HEADS):
        q = q_ref[:, h * LANES:(h + 1) * LANES]
        zero = jnp.zeros_like(q)
        qs.append((jnp.where(lane < A_DH, q, zero), jnp.where(lane >= A_DH, q, zero)))

    def tile(kt, carry, diag):
        k0 = pl.multiple_of(kt * tk, tk)
        dist = relf + (q0 - k0).astype(F32)
        out = []
        for h in range(A_HEADS):
            kk = kv_ref[pl.ds(k0, tk), 2 * h * LANES:(2 * h + 1) * LANES]
            v = kv_ref[pl.ds(k0, tk), (2 * h + 1) * LANES:(2 * h + 2) * LANES]
            bias = slopes_ref[h] * dist
            for c in range(2):
                s = _nt(qs[h][c], kk) - bias
                if diag:
                    s = jnp.where(dist >= 0, s, NEG)
                out.append(_online_update(carry[2 * h + c], s, v))
        return tuple(out)

    init1 = (jnp.full((tq, 1), NEG, F32), jnp.zeros((tq, 1), F32), jnp.zeros((tq, A_VD), F32))
    n_full = q0 // tk
    carry = lax.fori_loop(0, n_full, lambda kt, c: tile(kt, c, False), (init1,) * (2 * A_HEADS))
    for d in range(max(tq // tk, 1)):
        carry = tile(n_full + d, carry, True)
    for h in range(A_HEADS):
        (_, l1, a1), (_, l2, a2) = carry[2 * h], carry[2 * h + 1]
        o = a1 / l1 - lam * (a2 / l2)
        y = o * lax.rsqrt(jnp.mean(o * o, axis=-1, keepdims=True) + NORM_EPS) * subln_ref[...]
        o_ref[:, h * LANES:(h + 1) * LANES] = (y * (1.0 - lam_init)).astype(BF16)


def _diff_attn_prompt(aq, akvb, lam, subln, b, s, lam_init):
    tq, tk = 256, 256
    assert s % tq == 0 and s % tk == 0 and (tq % tk == 0 or tk % tq == 0)
    aq = aq.reshape(b, s, 512)
    akvb = akvb.reshape(b, s, 1024)
    out = pl.pallas_call(
        functools.partial(_diff_prompt_kernel, tq=tq, tk=tk, lam_init=lam_init),
        grid=(b, s // tq),
        in_specs=[pl.BlockSpec(memory_space=pltpu.SMEM), pl.BlockSpec(memory_space=pltpu.SMEM),
                  pl.BlockSpec((None, tq, 512), lambda bi, qi: (bi, qi, 0)),
                  pl.BlockSpec((None, s, 1024), lambda bi, qi: (bi, 0, 0)),
                  pl.BlockSpec((1, A_VD), lambda bi, qi: (0, 0))],
        out_specs=pl.BlockSpec((None, tq, 512), lambda bi, qi: (bi, qi, 0)),
        out_shape=jax.ShapeDtypeStruct((b, s, 512), BF16),
        compiler_params=_params(("parallel", "arbitrary")),
        name="diff_attn_prompt",
    )(lam.reshape(1), jnp.asarray(_alibi(A_HEADS)), aq, akvb, subln.reshape(1, A_VD))
    return out.reshape(b * s, 512)


def _prep_compress(cmp_pe, cmp_w1, cmp_w2, v_first=True):
    w1r = cmp_w1.reshape(2, 2, CMP_STRIDE, B_DH, CMP_HID)
    base = w1r.transpose(2, 0, 3, 1, 4)
    wbig = jnp.zeros((CMP_STRIDE, B_KV, 2, B_DH, 2, B_KV, 2, CMP_HID), F32)
    w2big = jnp.zeros((B_KV, 2, CMP_HID, B_KV, 2, B_DH), F32)
    for g in range(B_KV):
        for e in range(2):
            wbig = wbig.at[:, g, e, :, :, g, e, :].set(base[:, e])
            w2big = w2big.at[g, e, :, g, (1 - e) if v_first else e, :].set(cmp_w2[e])
    wbig = wbig.reshape(CMP_STRIDE * 256, 1024).astype(BF16)
    w2big = w2big.reshape(512, 256).astype(BF16)
    per = cmp_pe.reshape(2, 2, CMP_STRIDE, B_DH)
    pec = jnp.broadcast_to(per.transpose(1, 2, 0, 3)[:, :, None], (2, CMP_STRIDE, B_KV, 2, B_DH))
    pec = jnp.pad(pec.reshape(2, CMP_STRIDE * 256), ((0, 6), (0, 0))).astype(BF16)
    return wbig, w2big, pec


def _compress_kernel(*refs, n_in, rows):
    x_refs = refs[-(2 * n_in + 5):-5]
    w1_ref, pe_ref, w2_ref, o_ref, prev_ref = refs[-5:]
    step = pl.program_id(1)
    m_each = rows // CMP_STRIDE

    @pl.when(step == 0)
    def _():
        prev_ref[...] = jnp.zeros_like(prev_ref)

    pieces = []
    for i in range(n_in):
        cols = [x_refs[2 * i + g][pl.ds(s, m_each, stride=CMP_STRIDE), :].astype(BF16)
                for s in range(CMP_STRIDE) for g in range(B_KV)]
        pieces.append(jnp.concatenate(cols, axis=1))
    c = pieces[0] if n_in == 1 else jnp.concatenate(pieces, axis=0)
    m = c.shape[0]
    p = jnp.dot(c, w1_ref[...], preferred_element_type=F32)
    pp = jnp.dot(pe_ref[...], w1_ref[...], preferred_element_type=F32)
    pe_term = pp[0:1, :512] + pp[1:2, 512:]
    p0 = p[:, :512]
    p1 = p[:, 512:]
    row = lax.broadcasted_iota(I32, (m, 512), 0)
    p0s = jnp.where(row == 0, prev_ref[7:8, :], pltpu.roll(p0, 1, 0))
    prev_ref[...] = p0[m - 8:m, :]
    act = _gelu(p0s + p1 + pe_term)
    o_ref[...] = jnp.dot(act.astype(BF16), w2_ref[...], preferred_element_type=F32).astype(BF16)


def _compress_prompt(ckv, b, s, wbig, w2big, pec):
    rows = min(s, 2048)
    assert s % rows == 0 and rows % (8 * CMP_STRIDE) == 0
    n_ch = s // CMP_STRIDE
    m = rows // CMP_STRIDE
    const = lambda shp: pl.BlockSpec(shp, lambda bi, st: (0, 0))
    return pl.pallas_call(
        functools.partial(_compress_kernel, n_in=1, rows=rows),
        grid=(b, s // rows),
        in_specs=[pl.BlockSpec((None, rows, LANES), lambda bi, st: (bi, st, 0)),
                  pl.BlockSpec((None, rows, LANES), lambda bi, st: (bi, st, 1)),
                  const(wbig.shape), const(pec.shape), const(w2big.shape)],
        out_specs=pl.BlockSpec((None, m, 256), lambda bi, st: (bi, st, 0)),
        out_shape=jax.ShapeDtypeStruct((b, n_ch, 256), BF16),
        scratch_shapes=[pltpu.VMEM((8, 512), F32)],
        compiler_params=_params(("parallel", "arbitrary")),
        name="nsa_compress_prompt",
    )(ckv.reshape(b, s, 256), ckv.reshape(b, s, 256), wbig, pec, w2big)


def _rank_rows(score, n_rows):
    jrow = lax.broadcasted_iota(I32, score.shape, 0)
    rank = jnp.zeros(score.shape, I32)
    for j2 in range(n_rows):
        sj = score[j2:j2 + 1, :]
        rank = rank + jnp.where(jrow > j2, (sj >= score).astype(I32), (sj > score).astype(I32))
    return rank


def _nsa_prompt_kernel(slopes_ref, q_ref, ckv_ref, ske_ref, skv_ref, wkv_ref, gates_ref, ovt_ref, onorm_ref,
                       o_ref, *, tq, tk, n_slc):
    g = pl.program_id(1)
    qi = pl.program_id(2)
    q0 = qi * tq
    mrows = B_REP * tq
    n_ch = ckv_ref.shape[0]
    q = q_ref[...].reshape(mrows, LANES)
    rowi = lax.broadcasted_iota(I32, (mrows, 1), 0)
    slope_col = jnp.zeros((mrows, 1), F32)
    for r in range(B_REP):
        slope_col = jnp.where(rowi // tq == r, slopes_ref[g * B_REP + r], slope_col)
    qpos_i = q0 + rowi % tq
    qpos = qpos_i.astype(F32)
    lane = lax.broadcasted_iota(I32, (mrows, LANES), 1)

    ckv = ckv_ref[...]
    s = _nt(q, ckv)
    mcol = lax.broadcasted_iota(I32, (1, n_ch), 1)
    cpos = mcol * CMP_STRIDE + (CMP_STRIDE - 1)
    valid = (mcol >= 1) & (cpos <= qpos_i)
    s = jnp.where(valid, s - slope_col * (qpos - cpos.astype(F32)), NEG)
    p = jnp.exp(s - jnp.max(s, axis=1, keepdims=True))
    p = jnp.where(qpos_i >= CMP_LEN - 1, p / jnp.sum(p, axis=1, keepdims=True), 0.0)
    o_c = jnp.dot(p.astype(BF16), ckv, preferred_element_type=F32)
    psum = p[0:tq] + p[tq:2 * tq] + p[2 * tq:3 * tq] + p[3 * tq:4 * tq]
    hi = psum.astype(BF16)
    lo = (psum - hi.astype(F32)).astype(BF16)
    imp_t = _nt(ovt_ref[...], hi) + _nt(ovt_ref[...], lo)

    jrow = lax.broadcasted_iota(I32, (LANES, tq), 0)
    jcur = (q0 + lax.broadcasted_iota(I32, (LANES, tq), 1)) // SEL_BLOCK
    forced = (jrow == 0) | (jrow == jcur) | (jrow == jcur - 1)
    score = jnp.where(jrow > jcur, -FORCED, jnp.where(forced, FORCED, imp_t))
    rank = _rank_rows(score, n_slc)
    neg = jnp.where(rank < N_SEL, 0.0, NEG).T
    neg4 = jnp.concatenate([neg] * B_REP, axis=0).astype(BF16)
    q_aug = jnp.where(lane < B_DH, neg4, q)

    def attend(carry, qq, kk, vv, k0, n, mask_fn):
        kpos_i = k0 + lax.broadcasted_iota(I32, (1, n), 1)
        sc = _nt(qq, kk) - slope_col * (qpos - kpos_i.astype(F32))
        if mask_fn is not None:
            sc = jnp.where(mask_fn(kpos_i), sc, NEG)
        return _online_update(carry, sc, vv)

    init = (jnp.full((mrows, 1), NEG, F32), jnp.zeros((mrows, 1), F32), jnp.zeros((mrows, LANES), F32))

    def sel_tile(kt, carry, diag):
        k0 = pl.multiple_of(kt * tk, tk)
        return attend(carry, q_aug, ske_ref[pl.ds(k0, tk), :], skv_ref[pl.ds(k0, tk), :], k0, tk,
                      (lambda kp: kp <= qpos_i) if diag else None)

    n_full = q0 // tk
    carry = lax.fori_loop(0, n_full, lambda kt, c: sel_tile(kt, c, False), init)
    _, l_s, a_s = sel_tile(n_full, carry, True)

    wlen = min(WINDOW + tq, wkv_ref.shape[0])
    k0w = pl.multiple_of(jnp.maximum(q0 + tq - wlen, 0), tq)
    kvw = wkv_ref[pl.ds(k0w, wlen), :]

    def wmask(kp):
        d = qpos_i - kp
        return (d >= 0) & (d < WINDOW)

    _, l_w, a_w = attend(init, q, kvw, kvw, k0w, wlen, wmask)

    gates = gates_ref[...]
    glane = lax.broadcasted_iota(I32, (tq, LANES), 1)

    def gate_col(branch):
        cols = [jnp.sum(jnp.where(glane == branch * B_HEADS + g * B_REP + r, gates, 0.0), axis=1, keepdims=True)
                for r in range(B_REP)]
        return jnp.concatenate(cols, axis=0)

    o = gate_col(0) * o_c + gate_col(1) * (a_s / l_s) + gate_col(2) * (a_w / l_w)
    o = jnp.where(lane < B_DH, o, 0.0)
    y = o * lax.rsqrt(jnp.sum(o * o, axis=1, keepdims=True) * (1.0 / B_DH) + NORM_EPS) * onorm_ref[...]
    o_ref[...] = y.reshape(B_REP, tq, LANES)[:, :, :B_DH].astype(BF16)


def _overlap_t(n_rows, n_slc):
    m = np.arange(n_rows)[None, :]
    start = (m - 1) * CMP_STRIDE
    j = np.arange(LANES)[:, None]
    ov = (start < (j + 1) * SEL_BLOCK) & (start + CMP_LEN > j * SEL_BLOCK) & (m >= 1) & (j < n_slc)
    return jnp.asarray(ov.astype(np.float32), BF16)


def _nsa_prompt(bq, ckvc, skvb, wkvb, gates, out_norm, b, s):
    tq, tk = 128, 256
    tk = min(tk, s)
    assert s % tk == 0 and tk % tq == 0 and s >= WINDOW
    n_ch = s // CMP_STRIDE
    n_slc = s // SEL_BLOCK
    assert n_slc <= B_DH
    blk = np.arange(s) // SEL_BLOCK
    onehot = jnp.asarray((blk[:, None] == np.arange(LANES)[None, :]).astype(np.float32), BF16)
    skvb = skvb.reshape(b, s, 256)
    lane = jnp.arange(256) % LANES
    ske = jnp.where(lane < B_DH, jnp.concatenate([onehot, onehot], axis=1)[None], skvb)
    onorm = jnp.pad(out_norm, (0, LANES - B_DH)).reshape(1, LANES)
    grp = lambda n: pl.BlockSpec((None, n, LANES), lambda bi, g, qi: (bi, 0, g))
    out = pl.pallas_call(
        functools.partial(_nsa_prompt_kernel, tq=tq, tk=tk, n_slc=n_slc),
        grid=(b, B_KV, s // tq),
        in_specs=[pl.BlockSpec(memory_space=pltpu.SMEM),
                  pl.BlockSpec((B_REP, None, tq, LANES), lambda bi, g, qi: (g, bi, qi, 0)),
                  grp(n_ch), grp(s), grp(s), grp(s),
                  pl.BlockSpec((None, tq, LANES), lambda bi, g, qi: (bi, qi, 0)),
                  pl.BlockSpec((LANES, n_ch), lambda bi, g, qi: (0, 0)),
                  pl.BlockSpec((1, LANES), lambda bi, g, qi: (0, 0))],
        out_specs=pl.BlockSpec((B_REP, None, tq, B_DH), lambda bi, g, qi: (g, bi, qi, 0)),
        out_shape=jax.ShapeDtypeStruct((B_HEADS, b, s, B_DH), BF16),
        compiler_params=_params(("parallel", "parallel", "arbitrary")),
        name="nsa_prompt",
    )(jnp.asarray(_alibi(B_HEADS)), bq.reshape(B_HEADS, b, s, LANES), ckvc, ske, skvb,
      wkvb.reshape(b, s, 256), gates.reshape(b, s, LANES), _overlap_t(n_ch, n_slc), onorm)
    return out.transpose(1, 2, 0, 3).reshape(b * s, 512)


def _topk_rows(x, k, payload=None):
    n = x.shape[0]
    rows = lax.broadcasted_iota(I32, x.shape, 0)
    vals, outs = [], []
    cur = x
    for _ in range(k):
        m = jnp.max(cur, axis=0, keepdims=True)
        idx = jnp.min(jnp.where(cur == m, rows, n), axis=0, keepdims=True)
        hit = rows == idx
        vals.append(m)
        outs.append(idx if payload is None else jnp.sum(jnp.where(hit, payload, 0), axis=0, keepdims=True))
        cur = jnp.where(hit, -jnp.inf, cur)
    return jnp.concatenate(vals, axis=0), jnp.concatenate(outs, axis=0)


def _product_key_candidates(hv, hi):
    half = PEER_TOPK // 2
    vals, eids = [], []
    for a in range(half):
        nb = PEER_TOPK // (a + 1)
        rows_b = PEER_TOPK if a == 0 else half
        v = hv[0][a:a + 1] + hv[1][0:rows_b]
        if nb < rows_b:
            v = jnp.where(lax.broadcasted_iota(I32, v.shape, 0) < nb, v, -jnp.inf)
        vals.append(v)
        eids.append(hi[0][a:a + 1] * PEER_NKEYS + hi[1][0:rows_b])
    vals.append(hv[0][half:PEER_TOPK] + hv[1][0:1])
    eids.append(hi[0][half:PEER_TOPK] * PEER_NKEYS + hi[1][0:1])
    return jnp.concatenate(vals, axis=0), jnp.concatenate(eids, axis=0)


def _route_kernel(x_ref, oa_ref, ob_ref, wo_ref, g_ref, wq_ref, keys_ref, x1_ref, xn_ref, eid_ref, gate_ref):
    x1 = (x_ref[...] + jnp.dot(oa_ref[...], wo_ref[0:512, :], preferred_element_type=F32)
          + jnp.dot(ob_ref[...], wo_ref[512:1024, :], preferred_element_type=F32))
    x1_ref[...] = x1
    xn = x1 * lax.rsqrt(jnp.mean(x1 * x1, axis=-1, keepdims=True) + NORM_EPS) * g_ref[...]
    xn_ref[...] = xn
    q = jnp.dot(xn.astype(BF16), wq_ref[...], preferred_element_type=F32).astype(BF16)
    for h in range(PEER_HEADS):
        hv, hi = [], []
        for p in range(2):
            c0 = (h * 2 + p) * LANES
            sc = _nt(keys_ref[p], q[:, c0:c0 + LANES])
            v, i = _topk_rows(sc, PEER_TOPK)
            hv.append(v)
            hi.append(i)
        comb, cand_eid = _product_key_candidates(hv, hi)
        top, eid = _topk_rows(comb, PEER_TOPK, payload=cand_eid)
        e = jnp.exp(top - jnp.max(top, axis=0, keepdims=True))
        eid_ref[h * PEER_TOPK:(h + 1) * PEER_TOPK, :] = eid
        gate_ref[h * PEER_TOPK:(h + 1) * PEER_TOPK, :] = e / jnp.sum(e, axis=0, keepdims=True)


def _route(x, oa, ob, w_out, norm_ffn, peer_wq, peer_subkeys):
    t = x.shape[0]
    tm = 256
    assert t % tm == 0
    nt = t // tm
    row = lambda n: pl.BlockSpec((tm, n), lambda i: (i, 0))
    const = lambda shp: pl.BlockSpec(shp, lambda i: (0,) * len(shp))
    tr = pl.BlockSpec((None, PEER_HEADS * PEER_TOPK, tm), lambda i: (i, 0, 0))
    x1, xn, eid_t, gate_t = pl.pallas_call(
        _route_kernel,
        grid=(nt,),
        in_specs=[row(D_MODEL), row(512), row(512), const((D_MODEL, D_MODEL)), const((1, D_MODEL)),
                  const((D_MODEL, PEER_HEADS * PEER_DK)), const((2, PEER_NKEYS, PEER_DK // 2))],
        out_specs=(row(D_MODEL), row(D_MODEL), tr, tr),
        out_shape=(jax.ShapeDtypeStruct((t, D_MODEL), F32), jax.ShapeDtypeStruct((t, D_MODEL), F32),
                   jax.ShapeDtypeStruct((nt, 128, tm), I32), jax.ShapeDtypeStruct((nt, 128, tm), F32)),
        compiler_params=_params(("parallel",)),
        name="out_proj_route",
    )(x, oa, ob, w_out.astype(BF16), norm_ffn.reshape(1, D_MODEL), peer_wq.astype(BF16), peer_subkeys.astype(BF16))
    eid = eid_t.transpose(0, 2, 1).reshape(t, 128)
    gate = gate_t.transpose(0, 2, 1).reshape(t, 128)
    return x1, xn, eid, gate


_N_PAIRS = PEER_HEADS * PEER_TOPK
_TILE_ROWS = 16


_KROWS = _N_PAIRS * _TILE_ROWS


def _pack_table(tab):
    e = tab.shape[0]
    t = tab.astype(BF16).reshape(e // 2, 2, 8, LANES).transpose(0, 2, 3, 1)
    return lax.bitcast_convert_type(t, jnp.uint32).reshape(e * 4, LANES)


def _tile(tab_ref, r8):
    return pltpu.bitcast(tab_ref[pl.ds(pl.multiple_of(r8, 8), 8), :], BF16)


def _gather_pair(tab_ref, row_ref, ta, tb_):
    ga = jnp.concatenate([_tile(tab_ref, row_ref[ta, j]) for j in range(_N_PAIRS)], axis=0)
    gb = jnp.concatenate([_tile(tab_ref, row_ref[tb_, j]) for j in range(_N_PAIRS)], axis=0)
    return jnp.concatenate([ga, gb], axis=1)


def _diag_mask():
    return ((lax.broadcasted_iota(I32, (8, _KROWS), 1) % _TILE_ROWS) // 2) == lax.broadcasted_iota(I32, (8, _KROWS), 0)


def _peer_hidden_kernel(row_ref, x_ref, par_ref, gate_ref, sum_ref, u_ref, o_ref, d_ref, *, tb):
    diag = _diag_mask()
    zero = jnp.zeros((8, LANES), BF16)

    def pair(i, carry):
        ta = 2 * i
        xa = x_ref[ta].astype(BF16)
        xb = x_ref[ta + 1].astype(BF16)
        lhs = jnp.concatenate([jnp.concatenate([xa, zero], axis=1), jnp.concatenate([zero, xb], axis=1)], axis=0)
        res = _nt(lhs, _gather_pair(u_ref, row_ref, ta, ta + 1))
        d_ref[pl.ds(ta, 1), :] = jnp.sum(jnp.where(diag, res[0:8], 0.0), axis=0, keepdims=True)
        d_ref[pl.ds(ta + 1, 1), :] = jnp.sum(jnp.where(diag, res[8:16], 0.0), axis=0, keepdims=True)
        return carry

    lax.fori_loop(0, tb // 2, pair, 0, unroll=4)
    d = d_ref[...]
    hi = d.astype(BF16)
    lo = (d - hi.astype(F32)).astype(BF16)
    hl = (jnp.dot(hi, sum_ref[...], preferred_element_type=F32)
          + jnp.dot(lo, sum_ref[...], preferred_element_type=F32))
    hdn = jnp.where(par_ref[...] == 1, hl[:, _N_PAIRS:], hl[:, :_N_PAIRS])
    o_ref[...] = gate_ref[...] * _gelu(hdn)


def _peer_out_kernel(row_ref, par_ref, w_ref, x1_ref, gain_ref, exlo_ref, exhi_ref, v_ref, o_ref, wl_ref, *, tb):
    par = par_ref[...].astype(F32)
    w = w_ref[...]
    wl_ref[...] = (jnp.dot((w * (1.0 - par)).astype(BF16), exlo_ref[...], preferred_element_type=F32)
                   + jnp.dot((w * par).astype(BF16), exhi_ref[...], preferred_element_type=F32))
    diag = _diag_mask()

    def finish(t, acc):
        x2 = x1_ref[t] + acc
        ss = jnp.sum(jnp.sum(x2 * x2, axis=1, keepdims=True), axis=0, keepdims=True)
        o_ref[t] = x2 * lax.rsqrt(ss * (1.0 / D_MODEL) + NORM_EPS) * gain_ref[...]

    def pair(i, carry):
        ta = 2 * i
        lhs = jnp.concatenate(
            [jnp.where(diag, jnp.broadcast_to(wl_ref[pl.ds(t, 1), :], (8, _KROWS)), 0.0) for t in (ta, ta + 1)],
            axis=0).astype(BF16)
        res = jnp.dot(lhs, _gather_pair(v_ref, row_ref, ta, ta + 1), preferred_element_type=F32)
        finish(ta, res[0:8, 0:LANES])
        finish(ta + 1, res[8:16, LANES:2 * LANES])
        return carry

    lax.fori_loop(0, tb // 2, pair, 0, unroll=4)


def _peer_consts():
    col = np.arange(_KROWS)
    own = (col[None, :] // _TILE_ROWS) == np.arange(_N_PAIRS)[:, None]
    half = col[None, :] % 2
    exlo = (own & (half == 0)).astype(np.float32)
    exhi = (own & (half == 1)).astype(np.float32)
    summat = np.concatenate([exlo.T, exhi.T], axis=1)
    return jnp.asarray(exlo, BF16), jnp.asarray(exhi, BF16), jnp.asarray(summat, BF16)


def _peer_experts(x1, xn, eid, gate, u_tab, v_tab, norm_final):
    t = x1.shape[0]
    tb = 128
    assert t % tb == 0
    smem = pl.BlockSpec((tb, _N_PAIRS), lambda i: (i, 0), memory_space=pltpu.SMEM)
    vrow = pl.BlockSpec((tb, _N_PAIRS), lambda i: (i, 0))
    tok = pl.BlockSpec((tb, 8, LANES), lambda i: (i, 0, 0))
    table = pl.BlockSpec(memory_space=pltpu.VMEM)
    const = lambda shp: pl.BlockSpec(shp, lambda i: (0,) * len(shp))
    row = (eid >> 1) * 8
    par = eid & 1
    exlo, exhi, summat = _peer_consts()
    w = pl.pallas_call(
        functools.partial(_peer_hidden_kernel, tb=tb),
        grid=(t // tb,),
        in_specs=[smem, tok, vrow, vrow, const(summat.shape), table],
        out_specs=vrow,
        out_shape=jax.ShapeDtypeStruct((t, _N_PAIRS), F32),
        scratch_shapes=[pltpu.VMEM((tb, _KROWS), F32)],
        compiler_params=_params(("arbitrary",)),
        name="peer_hidden",
    )(row, xn.reshape(t, 8, LANES), par, gate, summat, u_tab)
    y = pl.pallas_call(
        functools.partial(_peer_out_kernel, tb=tb),
        grid=(t // tb,),
        in_specs=[smem, vrow, vrow, tok, const((8, LANES)), const(exlo.shape), const(exhi.shape), table],
        out_specs=tok,
        out_shape=jax.ShapeDtypeStruct((t, 8, LANES), F32),
        scratch_shapes=[pltpu.VMEM((tb, _KROWS), F32)],
        compiler_params=_params(("arbitrary",)),
        name="peer_out",
    )(row, par, w, x1.reshape(t, 8, LANES), norm_final.reshape(8, LANES), exlo, exhi, v_tab)
    return y.reshape(t, D_MODEL)


_PAGES_PER_STEP = 8


def _head_rows(n_rows, per_head, values_ref, base):
    rowi = lax.broadcasted_iota(I32, (n_rows, 1), 0)
    col = jnp.zeros((n_rows, 1), F32)
    for h in range(n_rows // per_head):
        col = jnp.where(rowi // per_head == h, values_ref[base + h], col)
    return col


def _diff_sample_kernel(pt_ref, lam_ref, slopes_ref, *refs, nq, past, page, lam_init):
    del pt_ref
    k_pages = refs[0:2 * _PAGES_PER_STEP:2]
    v_pages = refs[1:2 * _PAGES_PER_STEP:2]
    wq_ref, new_ref, subln_ref, o_ref, m_ref, l_ref, acc_ref = refs[2 * _PAGES_PER_STEP:]
    st = pl.program_id(1)
    hr = 2 * nq
    n_rows = A_HEADS * hr
    slope_col = _head_rows(n_rows, hr, slopes_ref, 0)
    rowi = lax.broadcasted_iota(I32, (n_rows, 1), 0)
    qpos_i = past + rowi % nq
    qpos = qpos_i.astype(F32)
    wq = wq_ref[...]

    @pl.when(st == 0)
    def _():
        m_ref[...] = jnp.full_like(m_ref, NEG)
        l_ref[...] = jnp.zeros_like(l_ref)
        acc_ref[...] = jnp.zeros_like(acc_ref)

    def update(ks, vs, k0, n, mask):
        kpos_i = k0 + lax.broadcasted_iota(I32, (1, n), 1)
        s = jnp.concatenate([_nt(wq[h * hr:(h + 1) * hr], ks[h]) for h in range(A_HEADS)], axis=0)
        s = s - slope_col * (qpos - kpos_i.astype(F32))
        if mask is not None:
            s = jnp.where(mask(kpos_i), s, NEG)
        m = m_ref[...]
        mn = jnp.maximum(m, jnp.max(s, axis=1, keepdims=True))
        c = jnp.exp(m - mn)
        p = jnp.exp(s - mn)
        pb = p.astype(BF16)
        pv = jnp.concatenate([jnp.dot(pb[h * hr:(h + 1) * hr], vs[h], preferred_element_type=F32)
                              for h in range(A_HEADS)], axis=0)
        m_ref[...] = mn
        l_ref[...] = l_ref[...] * c + jnp.sum(p, axis=1, keepdims=True)
        acc_ref[...] = acc_ref[...] * c + pv

    def gather(page_refs, h):
        return jnp.concatenate([r[:, h, :].astype(BF16) for r in page_refs], axis=0)

    update([gather(k_pages, h) for h in range(A_HEADS)], [gather(v_pages, h) for h in range(A_HEADS)],
           st * _PAGES_PER_STEP * page, _PAGES_PER_STEP * page, None)

    @pl.when(st == pl.num_programs(1) - 1)
    def _():
        new = jnp.concatenate([new_ref[...], jnp.zeros((LANES - nq, new_ref.shape[1]), BF16)], axis=0)
        update([new[:, 2 * h * LANES:(2 * h + 1) * LANES] for h in range(A_HEADS)],
               [new[:, (2 * h + 1) * LANES:(2 * h + 2) * LANES] for h in range(A_HEADS)],
               past, LANES, lambda kp: kp <= qpos_i)
        o = acc_ref[...] / l_ref[...]
        lam = lam_ref[0]
        for h in range(A_HEADS):
            oh = o[h * hr:h * hr + nq] - lam * o[h * hr + nq:(h + 1) * hr]
            y = oh * lax.rsqrt(jnp.mean(oh * oh, axis=-1, keepdims=True) + NORM_EPS) * subln_ref[...]
            o_ref[:, h * A_VD:(h + 1) * A_VD] = (y * (1.0 - lam_init)).astype(BF16)


def _diff_attn_sample(aq, akvb, cache_a, page_table, lam, subln, db, nq, lam_init):
    n_phys, page = cache_a.shape[1], cache_a.shape[2]
    n_pages = page_table.shape[1]
    past = n_pages * page
    assert n_pages % _PAGES_PER_STEP == 0 and nq == 8
    n_rows = A_HEADS * 2 * nq
    q5 = aq.reshape(db, nq, A_HEADS, 2, A_DH).transpose(0, 2, 3, 1, 4)
    wq = jnp.einsum('bhcqd,cC->bhcqCd', q5, jnp.eye(2, dtype=BF16)).reshape(db, n_rows, 2 * A_DH)
    specs = []
    for i in range(_PAGES_PER_STEP):
        for half in range(2):
            specs.append(pl.BlockSpec((None, None, page, A_HEADS, LANES),
                                      lambda b, st, pt, i=i, half=half: (0, pt[b, st * _PAGES_PER_STEP + i], 0, 0, half)))
    per_b = lambda r, c: pl.BlockSpec((None, r, c), lambda b, st, pt: (b, 0, 0))
    grid_spec = pltpu.PrefetchScalarGridSpec(
        num_scalar_prefetch=1,
        grid=(db, n_pages // _PAGES_PER_STEP),
        in_specs=[pl.BlockSpec(memory_space=pltpu.SMEM), pl.BlockSpec(memory_space=pltpu.SMEM)] + specs
        + [per_b(n_rows, 2 * A_DH), per_b(nq, 1024), pl.BlockSpec((1, A_VD), lambda b, st, pt: (0, 0))],
        out_specs=per_b(nq, 512),
        scratch_shapes=[pltpu.VMEM((n_rows, 1), F32), pltpu.VMEM((n_rows, 1), F32), pltpu.VMEM((n_rows, A_VD), F32)],
    )
    out = pl.pallas_call(
        functools.partial(_diff_sample_kernel, nq=nq, past=past, page=page, lam_init=lam_init),
        grid_spec=grid_spec,
        out_shape=jax.ShapeDtypeStruct((db, nq, 512), BF16),
        compiler_params=_params(("parallel", "arbitrary")),
        name="diff_attn_sample",
    )(page_table, lam.reshape(1), jnp.asarray(_alibi(A_HEADS)), *([cache_a] * (2 * _PAGES_PER_STEP)), wq,
      akvb.reshape(db, nq, 1024), subln.reshape(1, A_VD))
    return out.reshape(db * nq, 512)


def _compress_sample(cache_cmp, page_table, wbig, w2big, pec, db):
    n_phys, page = cache_cmp.shape[1], cache_cmp.shape[2]
    n_pages = page_table.shape[1]
    assert n_pages % _PAGES_PER_STEP == 0 and page % (8 * CMP_STRIDE) == 0
    m = _PAGES_PER_STEP * page // CMP_STRIDE
    n_ch = n_pages * page // CMP_STRIDE
    cache = cache_cmp.reshape(n_phys, page, 256)
    specs = []
    for i in range(_PAGES_PER_STEP):
        for g in range(B_KV):
            specs.append(pl.BlockSpec((None, page, LANES),
                                      lambda b, st, pt, i=i, g=g: (pt[b, st * _PAGES_PER_STEP + i], 0, g)))
    const = lambda shp: pl.BlockSpec(shp, lambda b, st, pt: (0, 0))
    grid_spec = pltpu.PrefetchScalarGridSpec(
        num_scalar_prefetch=1,
        grid=(db, n_pages // _PAGES_PER_STEP),
        in_specs=specs + [const(wbig.shape), const(pec.shape), const(w2big.shape)],
        out_specs=pl.BlockSpec((None, m, 256), lambda b, st, pt: (b, st, 0)),
        scratch_shapes=[pltpu.VMEM((8, 512), F32)],
    )
    return pl.pallas_call(
        functools.partial(_compress_kernel, n_in=_PAGES_PER_STEP, rows=page),
        grid_spec=grid_spec,
        out_shape=jax.ShapeDtypeStruct((db, n_ch, 256), BF16),
        compiler_params=_params(("parallel", "arbitrary")),
        name="nsa_compress_sample",
    )(page_table, *([cache] * (2 * _PAGES_PER_STEP)), wbig, pec, w2big)


def _nsa_sample_kernel(pt_ref, slopes_ref, *refs, nq, past, page, n_slc):
    del pt_ref
    pages = refs[:_PAGES_PER_STEP]
    (wq_ref, ckv_ref, win_ref, snew_ref, wnew_ref, gates_ref, ovt_ref, onorm_ref, o_ref,
     neg_ref, oc_ref, m_ref, l_ref, acc_ref) = refs[_PAGES_PER_STEP:]
    st = pl.program_id(1)
    n_rows = B_HEADS * nq
    n_j = neg_ref.shape[1]
    slope_col = _head_rows(n_rows, nq, slopes_ref, 0)
    rowi = lax.broadcasted_iota(I32, (n_rows, 1), 0)
    qpos_i = past + rowi % nq
    qpos = qpos_i.astype(F32)
    wq = wq_ref[...]
    init = (jnp.full((n_rows, 1), NEG, F32), jnp.zeros((n_rows, 1), F32), jnp.zeros((n_rows, 256), F32))

    def pad_rows(x):
        return jnp.concatenate([x, jnp.zeros((LANES - x.shape[0], x.shape[1]), x.dtype)], axis=0)

    def scores(rows_bf, kpos_i):
        return _nt(wq, rows_bf) - slope_col * (qpos - kpos_i.astype(F32))

    def block_bias(blk_of_key):
        n = blk_of_key.shape[1]
        onehot = (lax.broadcasted_iota(I32, (n_j, n), 0) == blk_of_key).astype(BF16)
        return jnp.dot(neg_ref[...], onehot, preferred_element_type=F32)

    @pl.when(st == 0)
    def _():
        ckv = ckv_ref[...]
        n_ch = ckv.shape[0]
        mcol = lax.broadcasted_iota(I32, (1, n_ch), 1)
        cpos = mcol * CMP_STRIDE + (CMP_STRIDE - 1)
        valid = (mcol >= 1) & (cpos <= qpos_i)
        s = jnp.where(valid, scores(ckv, cpos), NEG)
        p = jnp.exp(s - jnp.max(s, axis=1, keepdims=True))
        p = jnp.where(qpos_i >= CMP_LEN - 1, p / jnp.sum(p, axis=1, keepdims=True), 0.0)
        oc_ref[...] = jnp.dot(p.astype(BF16), ckv, preferred_element_type=F32)
        grp = B_REP * nq
        psum = jnp.concatenate(
            [sum(p[g * grp + r * nq:g * grp + (r + 1) * nq] for r in range(B_REP)) for g in range(B_KV)], axis=0)
        hi = psum.astype(BF16)
        lo = (psum - hi.astype(F32)).astype(BF16)
        imp_t = _nt(ovt_ref[...], hi) + _nt(ovt_ref[...], lo)
        shp = imp_t.shape
        jrow = lax.broadcasted_iota(I32, shp, 0)
        jcur = (past + lax.broadcasted_iota(I32, shp, 1) % nq) // SEL_BLOCK
        forced = (jrow == 0) | (jrow == jcur) | (jrow == jcur - 1)
        score = jnp.where(jrow > jcur, -FORCED, jnp.where(forced, FORCED, imp_t))
        rank = _rank_rows(score, n_slc)
        neg_t = jnp.where(rank < N_SEL, 0.0, NEG)
        neg = jnp.concatenate([neg_t, jnp.zeros((n_j, LANES - shp[1]), F32)], axis=1).T
        neg = jnp.concatenate([neg[g * nq:(g + 1) * nq] for g in range(B_KV) for _ in range(B_REP)], axis=0)
        neg_ref[...] = neg.astype(BF16)
        m_ref[...], l_ref[...], acc_ref[...] = init

    rows = jnp.concatenate([pages[i][...].astype(BF16) for i in range(_PAGES_PER_STEP)], axis=0)
    n_keys = _PAGES_PER_STEP * page
    kpos_i = st * n_keys + lax.broadcasted_iota(I32, (1, n_keys), 1)
    m_ref[...], l_ref[...], acc_ref[...] = _online_update(
        (m_ref[...], l_ref[...], acc_ref[...]), scores(rows, kpos_i) + block_bias(kpos_i // SEL_BLOCK), rows)

    @pl.when(st == pl.num_programs(1) - 1)
    def _():
        npos_i = past + lax.broadcasted_iota(I32, (1, LANES), 1)
        causal = npos_i <= qpos_i
        snew = pad_rows(snew_ref[...].astype(BF16))
        s = jnp.where(causal, scores(snew, npos_i) + block_bias(npos_i // SEL_BLOCK), NEG)
        _, l_s, a_s = _online_update((m_ref[...], l_ref[...], acc_ref[...]), s, snew)
        win = win_ref[...].astype(BF16)
        wb = win.shape[0]
        wpos_i = past - wb + lax.broadcasted_iota(I32, (1, wb), 1)
        d = qpos_i - wpos_i
        cw = _online_update(init, jnp.where((d >= 0) & (d < WINDOW), scores(win, wpos_i), NEG), win)
        wnew = pad_rows(wnew_ref[...].astype(BF16))
        _, l_w, a_w = _online_update(cw, jnp.where(causal, scores(wnew, npos_i), NEG), wnew)
        gates = jnp.concatenate([gates_ref[...]] * B_HEADS, axis=0)
        glane = lax.broadcasted_iota(I32, (n_rows, LANES), 1)

        def gate_col(branch):
            return jnp.sum(jnp.where(glane == branch * B_HEADS + rowi // nq, gates, 0.0), axis=1, keepdims=True)

        o = gate_col(0) * oc_ref[...] + gate_col(1) * (a_s / l_s) + gate_col(2) * (a_w / l_w)
        lane = lax.broadcasted_iota(I32, (nq, LANES), 1)
        outs = []
        for hd in range(B_HEADS):
            g = hd // B_REP
            blk = jnp.where(lane >= B_DH, o[hd * nq:(hd + 1) * nq, g * LANES:(g + 1) * LANES], 0.0)
            y = blk * lax.rsqrt(jnp.sum(blk * blk, axis=1, keepdims=True) * (1.0 / B_DH) + NORM_EPS) * onorm_ref[...]
            outs.append(y[:, B_DH:])
        o_ref[...] = jnp.concatenate(outs, axis=1).astype(BF16)


def _overlap_rows(n_rows, n_slc, n_j):
    m = np.arange(n_rows)[None, :]
    start = (m - 1) * CMP_STRIDE
    j = np.arange(n_j)[:, None]
    ov = (start < (j + 1) * SEL_BLOCK) & (start + CMP_LEN > j * SEL_BLOCK) & (m >= 1) & (j < n_slc)
    return jnp.asarray(ov.astype(np.float32), BF16)


def _nsa_sample(bq, ckv, skv, wkv, gates, cache_cmp, cache_sel, state_win, page_table, cmp_pe, cmp_w1, cmp_w2,
                out_norm, db, nq):
    n_phys, page = cache_sel.shape[1], cache_sel.shape[2]
    n_pages = page_table.shape[1]
    past = n_pages * page
    wb = state_win.shape[2]
    assert nq == 8 and nq < CMP_STRIDE and past % SEL_BLOCK == 0 and wb == WINDOW and page % SEL_BLOCK == 0
    n_ch = past // CMP_STRIDE
    n_slc = -(-(past + nq) // SEL_BLOCK)
    n_j = -(-n_slc // LANES) * LANES
    n_rows = B_HEADS * nq
    wbig, w2big, pec = _prep_compress(cmp_pe, cmp_w1, cmp_w2, v_first=False)
    ckvc = _compress_sample(cache_cmp, page_table, wbig, w2big, pec, db)
    q = bq[:, :, B_DH:].reshape(B_KV, B_REP, db, nq, B_DH)
    wq = jnp.einsum('grbqd,gG->bgrqGd', q, jnp.eye(B_KV, dtype=BF16))
    wq = jnp.pad(wq, ((0, 0),) * 5 + ((0, B_DH),)).reshape(db, n_rows, 256)
    onorm = jnp.pad(out_norm, (B_DH, 0)).reshape(1, LANES)
    cache = cache_sel.reshape(n_phys, page, 256)
    page_spec = lambda i: pl.BlockSpec((None, page, 256), lambda b, st, pt: (pt[b, st * _PAGES_PER_STEP + i], 0, 0))
    per_b = lambda r, c: pl.BlockSpec((None, r, c), lambda b, st, pt: (b, 0, 0))
    const = lambda shp: pl.BlockSpec(shp, lambda b, st, pt: (0, 0))
    grid_spec = pltpu.PrefetchScalarGridSpec(
        num_scalar_prefetch=1,
        grid=(db, n_pages // _PAGES_PER_STEP),
        in_specs=[pl.BlockSpec(memory_space=pltpu.SMEM)] + [page_spec(i) for i in range(_PAGES_PER_STEP)]
        + [per_b(n_rows, 256), per_b(n_ch, 256), per_b(wb, 256), per_b(nq, 256), per_b(nq, 256), per_b(nq, LANES),
           const((n_j, n_ch)), const((1, LANES))],
        out_specs=per_b(nq, 512),
        scratch_shapes=[pltpu.VMEM((n_rows, n_j), BF16), pltpu.VMEM((n_rows, 256), F32), pltpu.VMEM((n_rows, 1), F32),
                        pltpu.VMEM((n_rows, 1), F32), pltpu.VMEM((n_rows, 256), F32)],
    )
    out = pl.pallas_call(
        functools.partial(_nsa_sample_kernel, nq=nq, past=past, page=page, n_slc=n_slc),
        grid_spec=grid_spec,
        out_shape=jax.ShapeDtypeStruct((db, nq, 512), BF16),
        compiler_params=_params(("parallel", "arbitrary")),
        name="nsa_sample",
    )(page_table, jnp.asarray(_alibi(B_HEADS)), *([cache] * _PAGES_PER_STEP), wq, ckvc,
      state_win.reshape(db, wb, 256), skv.reshape(db, nq, 256), wkv.reshape(db, nq, 256), gates.reshape(db, nq, LANES),
      _overlap_rows(n_ch, n_slc, n_j), onorm)
    return out.reshape(db * nq, 512)


def kernel(x_prompt, x_sample, cache_a, cache_cmp, cache_sel, state_win, page_table, norm_mix, w_in, diff_lambda,
           diff_subln, cmp_pe, cmp_w1, cmp_w2, nsa_out_norm, w_out, norm_ffn, peer_wq, peer_subkeys, peer_u, peer_v,
           norm_final):
    b, s, _ = x_prompt.shape
    db, nq, _ = x_sample.shape
    lam_init = 0.8 - 0.6 * math.exp(-0.3 * 0)
    dl = diff_lambda[0]
    lam = jnp.exp(jnp.sum(dl[0] * dl[1])) - jnp.exp(jnp.sum(dl[2] * dl[3])) + lam_init
    w = _prep_w_in(w_in[0])
    wbig, w2big, pec = _prep_compress(cmp_pe[0], cmp_w1[0], cmp_w2[0])

    xp = x_prompt.reshape(b * s, D_MODEL)
    aq, akv, akvb, bq, ckv, skv, wkv, skvb, wkvb, gates = _in_proj(xp, norm_mix[0], w)
    oa = _diff_attn_prompt(aq, akvb, lam, diff_subln[0], b, s, lam_init)
    ckvc = _compress_prompt(ckv, b, s, wbig, w2big, pec)
    ob = _nsa_prompt(bq, ckvc, skvb, wkvb, gates, nsa_out_norm[0], b, s)
    x1, xn, eid, gate = _route(xp, oa, ob, w_out[0], norm_ffn[0], peer_wq[0], peer_subkeys[0])
    u_tab = _pack_table(peer_u[0])
    v_tab = _pack_table(peer_v[0])
    y_prompt = _peer_experts(x1, xn, eid, gate, u_tab, v_tab, norm_final).reshape(b, s, D_MODEL)
    win = min(WINDOW, s)
    outs_p = (akv.reshape(1, b, s, A_HEADS, 256), ckv.reshape(1, b, s, B_KV, 128), skv.reshape(1, b, s, B_KV, 128),
              wkv.reshape(b, s, B_KV, 128)[None, :, s - win:])

    ts = db * nq
    xs = x_sample.reshape(ts, D_MODEL)
    saq, sakv, sakvb, sbq, sckv, sskv, swkv, _, _, sgates = _in_proj(xs, norm_mix[0], w)
    soa = _diff_attn_sample(saq, sakvb, cache_a, page_table, lam, diff_subln[0], db, nq, lam_init)
    sob = _nsa_sample(sbq, sckv, sskv, swkv, sgates, cache_cmp, cache_sel, state_win, page_table, cmp_pe[0],
                      cmp_w1[0], cmp_w2[0], nsa_out_norm[0], db, nq)
    sx1, sxn, seid, sgate = _route(xs, soa, sob, w_out[0], norm_ffn[0], peer_wq[0], peer_subkeys[0])
    y_sample = _peer_experts(sx1, sxn, seid, sgate, u_tab, v_tab, norm_final).reshape(db, nq, D_MODEL)
    new_win = jnp.concatenate([state_win[0][:, nq:], swkv.reshape(db, nq, B_KV, 128)], axis=1)
    return (y_prompt, y_sample, outs_p[0], sakv.reshape(1, db, nq, A_HEADS, 256), outs_p[1],
            sckv.reshape(1, db, nq, B_KV, 128), outs_p[2], sskv.reshape(1, db, nq, B_KV, 128), outs_p[3],
            new_win[None])
```

```python
import functools
import math

import jax
import jax.numpy as jnp
import numpy as np
from jax import lax
from jax.experimental import pallas as pl
from jax.experimental.pallas import tpu as pltpu

F32 = jnp.float32
BF16 = jnp.bfloat16
I32 = jnp.int32

D_MODEL = 1024
A_HEADS = 4
A_DH = 64
A_VD = 128
B_HEADS = 8
B_DH = 64
B_KV = 2
B_REP = 4
CMP_STRIDE = 16
CMP_LEN = 32
CMP_HID = 128
SEL_BLOCK = 64
N_SEL = 16
WINDOW = 512
PEER_HEADS = 8
PEER_NKEYS = 128
PEER_DK = 256
PEER_TOPK = 16
NORM_EPS = 1e-6
NEG = -1e30
FORCED = 1e9

VMEM_LIMIT_V7X = 52 * 1024 * 1024
LANES = 128

_C_AQ = 0
_C_AKV = 512
_C_BQ = 1536
_C_CKV = 2560
_C_SKV = 2816
_C_WKV = 3072
_C_GATE = 3328
_C_END = 3456


def _nt(a, b):
    return lax.dot_general(a, b, (((1,), (1,)), ((), ())), preferred_element_type=F32)


def _gelu(x):
    return 0.5 * x * (1.0 + lax.erf(x * 0.7071067811865476))


def _params(sem):
    return pltpu.CompilerParams(dimension_semantics=sem, vmem_limit_bytes=VMEM_LIMIT_V7X)


def _alibi(n):
    return np.asarray(2.0 ** (-8.0 * np.arange(1, n + 1) / n), dtype=np.float32)


def _prep_w_in(w_in):
    d = w_in.shape[0]
    aq = w_in[:, 0:512] * 0.125
    ak = w_in[:, 512:1024].reshape(d, A_HEADS, 2 * A_DH)
    av = w_in[:, 1024:1536].reshape(d, A_HEADS, A_VD)
    akv = jnp.concatenate([ak, av], axis=-1).reshape(d, 1024)
    bq = w_in[:, 1536:2048].reshape(d, B_HEADS, B_DH) * 0.125
    bq = jnp.concatenate([jnp.zeros_like(bq), bq], axis=-1).reshape(d, 1024)
    ckv = w_in[:, 2048:2304]
    skv = w_in[:, 2304:2560]
    wkv = w_in[:, 2560:2816]
    gt = w_in[:, 2816:2840].reshape(d, B_HEADS, 3).transpose(0, 2, 1).reshape(d, 24)
    gt = jnp.pad(gt, ((0, 0), (0, LANES - 24)))
    return jnp.concatenate([aq, akv, bq, ckv, skv, wkv, gt], axis=1).astype(BF16)


def _inproj_kernel(x_ref, g_ref, w_ref, aq_ref, akv_ref, akvb_ref, bq_ref, ckv_ref, skv_ref, wkv_ref,
                   skvb_ref, wkvb_ref, gate_ref):
    x = x_ref[...]
    xn = x * lax.rsqrt(jnp.mean(x * x, axis=-1, keepdims=True) + NORM_EPS) * g_ref[...]
    xb = xn.astype(BF16)

    def seg(a, b):
        return jnp.dot(xb, w_ref[:, a:b], preferred_element_type=F32)

    aq_ref[...] = seg(_C_AQ, _C_AKV).astype(BF16)
    akv = seg(_C_AKV, _C_BQ)
    akv_ref[...] = akv
    akvb_ref[...] = akv.astype(BF16)
    bq = seg(_C_BQ, _C_CKV).astype(BF16)
    for hd in range(B_HEADS):
        bq_ref[hd] = bq[:, hd * LANES:(hd + 1) * LANES]
    ckv_ref[...] = seg(_C_CKV, _C_SKV)
    for (a, fref, bref) in ((_C_SKV, skv_ref, skvb_ref), (_C_WKV, wkv_ref, wkvb_ref)):
        kv = seg(a, a + 2 * LANES)
        fref[...] = kv
        sw = jnp.concatenate([pltpu.roll(kv[:, g * LANES:(g + 1) * LANES], 64, 1) for g in range(B_KV)], axis=1)
        bref[...] = sw.astype(BF16)
    gate_ref[...] = jax.nn.sigmoid(seg(_C_GATE, _C_END))


def _in_proj(x, gain, w):
    t = x.shape[0]
    tm = 256
    assert t % tm == 0
    row = lambda n: pl.BlockSpec((tm, n), lambda i: (i, 0))
    out_shape = (
        jax.ShapeDtypeStruct((t, 512), BF16),
        jax.ShapeDtypeStruct((t, 1024), F32),
        jax.ShapeDtypeStruct((t, 1024), BF16),
        jax.ShapeDtypeStruct((B_HEADS, t, LANES), BF16),
        jax.ShapeDtypeStruct((t, 256), F32),
        jax.ShapeDtypeStruct((t, 256), F32),
        jax.ShapeDtypeStruct((t, 256), F32),
        jax.ShapeDtypeStruct((t, 256), BF16),
        jax.ShapeDtypeStruct((t, 256), BF16),
        jax.ShapeDtypeStruct((t, LANES), F32),
    )
    out_specs = (row(512), row(1024), row(1024), pl.BlockSpec((B_HEADS, tm, LANES), lambda i: (0, i, 0)),
                 row(256), row(256), row(256), row(256), row(256), row(LANES))
    return pl.pallas_call(
        _inproj_kernel,
        grid=(t // tm,),
        in_specs=[row(D_MODEL), pl.BlockSpec((1, D_MODEL), lambda i: (0, 0)),
                  pl.BlockSpec((D_MODEL, _C_END), lambda i: (0, 0))],
        out_specs=out_specs,
        out_shape=out_shape,
        compiler_params=_params(("parallel",)),
        name="in_proj",
    )(x, gain.reshape(1, D_MODEL), w)


def _online_update(carry, s, v):
    m, l, acc = carry
    mn = jnp.maximum(m, jnp.max(s, axis=1, keepdims=True))
    c = jnp.exp(m - mn)
    p = jnp.exp(s - mn)
    l = l * c + jnp.sum(p, axis=1, keepdims=True)
    acc = acc * c + jnp.dot(p.astype(BF16), v, preferred_element_type=F32)
    return mn, l, acc


def _diff_prompt_kernel(lam_ref, slopes_ref, q_ref, kv_ref, subln_ref, o_ref, *, tq, tk, lam_init):
    qi = pl.program_id(1)
    q0 = qi * tq
    lam = lam_ref[0]
    lane = lax.broadcasted_iota(I32, (tq, LANES), 1)
    rel = (lax.broadcasted_iota(I32, (tq, tk), 0) - lax.broadcasted_iota(I32, (tq, tk), 1))
    relf = rel.astype(F32)
    qs = []
    for h in range(A_HEADS):
        q = q_ref[:, h * LANES:(h + 1) * LANES]
        zero = jnp.zeros_like(q)
        qs.append((jnp.where(lane < A_DH, q, zero), jnp.where(lane >= A_DH, q, zero)))

    def tile(kt, carry, diag):
        k0 = pl.multiple_of(kt * tk, tk)
        dist = relf + (q0 - k0).astype(F32)
        out = []
        for h in range(A_HEADS):
            kk = kv_ref[pl.ds(k0, tk), 2 * h * LANES:(2 * h + 1) * LANES]
            v = kv_ref[pl.ds(k0, tk), (2 * h + 1) * LANES:(2 * h + 2) * LANES]
            bias = slopes_ref[h] * dist
            for c in range(2):
                s = _nt(qs[h][c], kk) - bias
                if diag:
                    s = jnp.where(dist >= 0, s, NEG)
                out.append(_online_update(carry[2 * h + c], s, v))
        return tuple(out)

    init1 = (jnp.full((tq, 1), NEG, F32), jnp.zeros((tq, 1), F32), jnp.zeros((tq, A_VD), F32))
    n_full = q0 // tk
    carry = lax.fori_loop(0, n_full, lambda kt, c: tile(kt, c, False), (init1,) * (2 * A_HEADS))
    for d in range(max(tq // tk, 1)):
        carry = tile(n_full + d, carry, True)
    for h in range(A_HEADS):
        (_, l1, a1), (_, l2, a2) = carry[2 * h], carry[2 * h + 1]
        o = a1 / l1 - lam * (a2 / l2)
        y = o * lax.rsqrt(jnp.mean(o * o, axis=-1, keepdims=True) + NORM_EPS) * subln_ref[...]
        o_ref[:, h * LANES:(h + 1) * LANES] = (y * (1.0 - lam_init)).astype(BF16)


def _diff_attn_prompt(aq, akvb, lam, subln, b, s, lam_init):
    tq, tk = 256, 256
    assert s % tq == 0 and s % tk == 0 and (tq % tk == 0 or tk % tq == 0)
    aq = aq.reshape(b, s, 512)
    akvb = akvb.reshape(b, s, 1024)
    out = pl.pallas_call(
        functools.partial(_diff_prompt_kernel, tq=tq, tk=tk, lam_init=lam_init),
        grid=(b, s // tq),
        in_specs=[pl.BlockSpec(memory_space=pltpu.SMEM), pl.BlockSpec(memory_space=pltpu.SMEM),
                  pl.BlockSpec((None, tq, 512), lambda bi, qi: (bi, qi, 0)),
                  pl.BlockSpec((None, s, 1024), lambda bi, qi: (bi, 0, 0)),
                  pl.BlockSpec((1, A_VD), lambda bi, qi: (0, 0))],
        out_specs=pl.BlockSpec((None, tq, 512), lambda bi, qi: (bi, qi, 0)),
        out_shape=jax.ShapeDtypeStruct((b, s, 512), BF16),
        compiler_params=_params(("parallel", "arbitrary")),
        name="diff_attn_prompt",
    )(lam.reshape(1), jnp.asarray(_alibi(A_HEADS)), aq, akvb, subln.reshape(1, A_VD))
    return out.reshape(b * s, 512)


def _prep_compress(cmp_pe, cmp_w1, cmp_w2, v_first=True):
    w1r = cmp_w1.reshape(2, 2, CMP_STRIDE, B_DH, CMP_HID)
    base = w1r.transpose(2, 0, 3, 1, 4)
    wbig = jnp.zeros((CMP_STRIDE, B_KV, 2, B_DH, 2, B_KV, 2, CMP_HID), F32)
    w2big = jnp.zeros((B_KV, 2, CMP_HID, B_KV, 2, B_DH), F32)
    for g in range(B_KV):
        for e in range(2):
            wbig = wbig.at[:, g, e, :, :, g, e, :].set(base[:, e])
            w2big = w2big.at[g, e, :, g, (1 - e) if v_first else e, :].set(cmp_w2[e])
    wbig = wbig.reshape(CMP_STRIDE * 256, 1024).astype(BF16)
    w2big = w2big.reshape(512, 256).astype(BF16)
    per = cmp_pe.reshape(2, 2, CMP_STRIDE, B_DH)
    pec = jnp.broadcast_to(per.transpose(1, 2, 0, 3)[:, :, None], (2, CMP_STRIDE, B_KV, 2, B_DH))
    pec = jnp.pad(pec.reshape(2, CMP_STRIDE * 256), ((0, 6), (0, 0))).astype(BF16)
    return wbig, w2big, pec


def _compress_kernel(*refs, n_in, rows):
    x_refs = refs[-(2 * n_in + 5):-5]
    m_each = rows // CMP_STRIDE
    _compress_core(lambda i, g, s: x_refs[2 * i + g][pl.ds(s, m_each, stride=CMP_STRIDE), :], n_in, *refs[-5:])


def _compress_sample_kernel(pt_ref, cache_ref, w1_ref, pe_ref, w2_ref, o_ref, prev_ref, buf_ref, sem, *, n_in, rows):
    def copies(b, p0, slot):
        return [pltpu.make_async_copy(cache_ref.at[0, pt_ref[b, p0 + i], :, g, :], buf_ref.at[slot, 2 * i + g],
                                      sem.at[slot])
                for i in range(n_in) for g in range(B_KV)]

    slot = _paged_fetch(pt_ref, sem, copies, n_in)
    m_each = rows // CMP_STRIDE
    _compress_core(lambda i, g, s: buf_ref[slot, 2 * i + g, pl.ds(s, m_each, stride=CMP_STRIDE), :], n_in,
                   w1_ref, pe_ref, w2_ref, o_ref, prev_ref)


def _compress_core(chunk_rows, n_in, w1_ref, pe_ref, w2_ref, o_ref, prev_ref):
    step = pl.program_id(1)

    @pl.when(step == 0)
    def _():
        prev_ref[...] = jnp.zeros_like(prev_ref)

    pieces = []
    for i in range(n_in):
        cols = [chunk_rows(i, g, s).astype(BF16) for s in range(CMP_STRIDE) for g in range(B_KV)]
        pieces.append(jnp.concatenate(cols, axis=1))
    c = pieces[0] if n_in == 1 else jnp.concatenate(pieces, axis=0)
    m = c.shape[0]
    p = jnp.dot(c, w1_ref[...], preferred_element_type=F32)
    pp = jnp.dot(pe_ref[...], w1_ref[...], preferred_element_type=F32)
    pe_term = pp[0:1, :512] + pp[1:2, 512:]
    p0 = p[:, :512]
    p1 = p[:, 512:]
    row = lax.broadcasted_iota(I32, (m, 512), 0)
    p0s = jnp.where(row == 0, prev_ref[7:8, :], pltpu.roll(p0, 1, 0))
    prev_ref[...] = p0[m - 8:m, :]
    act = _gelu(p0s + p1 + pe_term)
    o_ref[...] = jnp.dot(act.astype(BF16), w2_ref[...], preferred_element_type=F32).astype(BF16)


def _compress_prompt(ckv, b, s, wbig, w2big, pec):
    rows = min(s, 2048)
    assert s % rows == 0 and rows % (8 * CMP_STRIDE) == 0
    n_ch = s // CMP_STRIDE
    m = rows // CMP_STRIDE
    const = lambda shp: pl.BlockSpec(shp, lambda bi, st: (0, 0))
    return pl.pallas_call(
        functools.partial(_compress_kernel, n_in=1, rows=rows),
        grid=(b, s // rows),
        in_specs=[pl.BlockSpec((None, rows, LANES), lambda bi, st: (bi, st, 0)),
                  pl.BlockSpec((None, rows, LANES), lambda bi, st: (bi, st, 1)),
                  const(wbig.shape), const(pec.shape), const(w2big.shape)],
        out_specs=pl.BlockSpec((None, m, 256), lambda bi, st: (bi, st, 0)),
        out_shape=jax.ShapeDtypeStruct((b, n_ch, 256), BF16),
        scratch_shapes=[pltpu.VMEM((8, 512), F32)],
        compiler_params=_params(("parallel", "arbitrary")),
        name="nsa_compress_prompt",
    )(ckv.reshape(b, s, 256), ckv.reshape(b, s, 256), wbig, pec, w2big)


def _rank_rows(score, n_rows):
    jrow = lax.broadcasted_iota(I32, score.shape, 0)
    rank = jnp.zeros(score.shape, I32)
    for j2 in range(n_rows):
        sj = score[j2:j2 + 1, :]
        rank = rank + jnp.where(jrow > j2, (sj >= score).astype(I32), (sj > score).astype(I32))
    return rank


def _nsa_prompt_kernel(slopes_ref, q_ref, ckv_ref, ske_ref, skv_ref, wkv_ref, gates_ref, ovt_ref, onorm_ref,
                       o_ref, *, tq, tk, n_slc):
    qi = pl.program_id(1)
    q0 = qi * tq
    mrows = B_REP * tq
    n_ch = ckv_ref.shape[0]
    rowi = lax.broadcasted_iota(I32, (mrows, 1), 0)
    qpos_i = q0 + rowi % tq
    qpos = qpos_i.astype(F32)
    lane = lax.broadcasted_iota(I32, (mrows, LANES), 1)
    n_jr = -(-n_slc // 8) * 8
    jrow = lax.broadcasted_iota(I32, (n_jr, tq), 0)
    jcur = (q0 + lax.broadcasted_iota(I32, (n_jr, tq), 1)) // SEL_BLOCK
    forced = (jrow == 0) | (jrow == jcur) | (jrow == jcur - 1)
    mcol = lax.broadcasted_iota(I32, (1, n_ch), 1)
    cpos = jnp.where(mcol >= 1, mcol * CMP_STRIDE + (CMP_STRIDE - 1), 2 ** 30).astype(F32)

    def lanes_of(g):
        return slice(g * LANES, (g + 1) * LANES)

    def prepare(g):
        q = q_ref[g * B_REP:(g + 1) * B_REP].reshape(mrows, LANES)
        slope_col = jnp.zeros((mrows, 1), F32)
        for r in range(B_REP):
            slope_col = jnp.where(rowi // tq == r, slopes_ref[g * B_REP + r], slope_col)
        ckv = ckv_ref[:, lanes_of(g)]
        s = _nt(q, ckv)
        s = jnp.where(cpos <= qpos, s - slope_col * (qpos - cpos), NEG)
        p = jnp.exp(s - jnp.max(s, axis=1, keepdims=True))
        p = jnp.where(qpos_i >= CMP_LEN - 1, p / jnp.sum(p, axis=1, keepdims=True), 0.0)
        o_c = jnp.dot(p.astype(BF16), ckv, preferred_element_type=F32)
        psum = p[0:tq] + p[tq:2 * tq] + p[2 * tq:3 * tq] + p[3 * tq:4 * tq]
        hi = psum.astype(BF16)
        lo = (psum - hi.astype(F32)).astype(BF16)
        imp_t = _nt(ovt_ref[...], hi) + _nt(ovt_ref[...], lo)
        score = jnp.where(jrow > jcur, -FORCED, jnp.where(forced, FORCED, imp_t[0:n_jr]))
        neg = jnp.where(_rank_rows(score, n_slc) < N_SEL, 0.0, NEG)
        neg = jnp.concatenate([neg, jnp.zeros((LANES - n_jr, tq), F32)], axis=0).T
        neg4 = jnp.concatenate([neg] * B_REP, axis=0).astype(BF16)
        q_aug = jnp.where(lane < B_DH, neg4, q)
        return q, q_aug, slope_col, o_c

    grp = [prepare(g) for g in range(B_KV)]

    def attend(g, carry, qq, kk, vv, k0, n, mask_fn):
        kpos_i = k0 + lax.broadcasted_iota(I32, (1, n), 1)
        sc = _nt(qq, kk) - grp[g][2] * (qpos - kpos_i.astype(F32))
        if mask_fn is not None:
            sc = jnp.where(mask_fn(kpos_i), sc, NEG)
        return _online_update(carry, sc, vv)

    init = (jnp.full((mrows, 1), NEG, F32), jnp.zeros((mrows, 1), F32), jnp.zeros((mrows, LANES), F32))

    def sel_tile(kt, carries, diag):
        k0 = pl.multiple_of(kt * tk, tk)
        return tuple(attend(g, carries[g], grp[g][1], ske_ref[pl.ds(k0, tk), lanes_of(g)],
                            skv_ref[pl.ds(k0, tk), lanes_of(g)], k0, tk, (lambda kp: kp <= qpos_i) if diag else None)
                     for g in range(B_KV))

    n_full = q0 // tk
    carries = lax.fori_loop(0, n_full, lambda kt, c: sel_tile(kt, c, False), (init,) * B_KV)
    carries = sel_tile(n_full, carries, True)

    wlen = min(WINDOW + tq, wkv_ref.shape[0])
    k0w = pl.multiple_of(jnp.maximum(q0 + tq - wlen, 0), tq)

    def wmask(kp):
        return (qpos_i - kp).astype(jnp.uint32) < WINDOW

    gates = gates_ref[...]
    glane = lax.broadcasted_iota(I32, (tq, LANES), 1)
    for g in range(B_KV):
        q, _, _, o_c = grp[g]
        _, l_s, a_s = carries[g]
        kvw = wkv_ref[pl.ds(k0w, wlen), lanes_of(g)]
        _, l_w, a_w = attend(g, init, q, kvw, kvw, k0w, wlen, wmask)

        def gate_col(branch, g=g):
            cols = [jnp.sum(jnp.where(glane == branch * B_HEADS + g * B_REP + r, gates, 0.0), axis=1, keepdims=True)
                    for r in range(B_REP)]
            return jnp.concatenate(cols, axis=0)

        o = gate_col(0) * o_c + gate_col(1) * (a_s / l_s) + gate_col(2) * (a_w / l_w)
        o = jnp.where(lane < B_DH, o, 0.0)
        y = o * lax.rsqrt(jnp.sum(o * o, axis=1, keepdims=True) * (1.0 / B_DH) + NORM_EPS) * onorm_ref[...]
        o_ref[g * B_REP:(g + 1) * B_REP] = y.reshape(B_REP, tq, LANES)[:, :, :B_DH].astype(BF16)


def _overlap_t(n_rows, n_slc):
    m = np.arange(n_rows)[None, :]
    start = (m - 1) * CMP_STRIDE
    j = np.arange(LANES)[:, None]
    ov = (start < (j + 1) * SEL_BLOCK) & (start + CMP_LEN > j * SEL_BLOCK) & (m >= 1) & (j < n_slc)
    return jnp.asarray(ov.astype(np.float32), BF16)


def _nsa_prompt(bq, ckvc, skvb, wkvb, gates, out_norm, b, s):
    tq, tk = 128, 256
    tk = min(tk, s)
    assert s % tk == 0 and tk % tq == 0 and s >= WINDOW
    n_ch = s // CMP_STRIDE
    n_slc = s // SEL_BLOCK
    assert n_slc <= B_DH
    blk = np.arange(s) // SEL_BLOCK
    onehot = jnp.asarray((blk[:, None] == np.arange(LANES)[None, :]).astype(np.float32), BF16)
    skvb = skvb.reshape(b, s, 256)
    lane = jnp.arange(256) % LANES
    ske = jnp.where(lane < B_DH, jnp.concatenate([onehot, onehot], axis=1)[None], skvb)
    onorm = jnp.pad(out_norm, (0, LANES - B_DH)).reshape(1, LANES)
    per_b = lambda n: pl.BlockSpec((None, n, B_KV * LANES), lambda bi, qi: (bi, 0, 0))
    out = pl.pallas_call(
        functools.partial(_nsa_prompt_kernel, tq=tq, tk=tk, n_slc=n_slc),
        grid=(b, s // tq),
        in_specs=[pl.BlockSpec(memory_space=pltpu.SMEM),
                  pl.BlockSpec((B_HEADS, None, tq, LANES), lambda bi, qi: (0, bi, qi, 0)),
                  per_b(n_ch), per_b(s), per_b(s), per_b(s),
                  pl.BlockSpec((None, tq, LANES), lambda bi, qi: (bi, qi, 0)),
                  pl.BlockSpec((LANES, n_ch), lambda bi, qi: (0, 0)),
                  pl.BlockSpec((1, LANES), lambda bi, qi: (0, 0))],
        out_specs=pl.BlockSpec((B_HEADS, None, tq, B_DH), lambda bi, qi: (0, bi, qi, 0)),
        out_shape=jax.ShapeDtypeStruct((B_HEADS, b, s, B_DH), BF16),
        compiler_params=_params(("parallel", "arbitrary")),
        name="nsa_prompt",
    )(jnp.asarray(_alibi(B_HEADS)), bq.reshape(B_HEADS, b, s, LANES), ckvc, ske, skvb,
      wkvb.reshape(b, s, 256), gates.reshape(b, s, LANES), _overlap_t(n_ch, n_slc), onorm)
    return out.transpose(1, 2, 0, 3).reshape(b * s, 512)


def _topk_rows(x, k, payload=None):
    n = x.shape[0]
    rows = lax.broadcasted_iota(I32, x.shape, 0)
    vals, outs = [], []
    cur = x
    for _ in range(k):
        m = jnp.max(cur, axis=0, keepdims=True)
        idx = jnp.min(jnp.where(cur == m, rows, n), axis=0, keepdims=True)
        hit = rows == idx
        vals.append(m)
        outs.append(idx if payload is None else jnp.sum(jnp.where(hit, payload, 0), axis=0, keepdims=True))
        cur = jnp.where(hit, -jnp.inf, cur)
    return jnp.concatenate(vals, axis=0), jnp.concatenate(outs, axis=0)


def _product_key_candidates(hv, hi):
    half = PEER_TOPK // 2
    vals, eids = [], []
    for a in range(half):
        nb = PEER_TOPK // (a + 1)
        rows_b = PEER_TOPK if a == 0 else half
        v = hv[0][a:a + 1] + hv[1][0:rows_b]
        if nb < rows_b:
            v = jnp.where(lax.broadcasted_iota(I32, v.shape, 0) < nb, v, -jnp.inf)
        vals.append(v)
        eids.append(hi[0][a:a + 1] * PEER_NKEYS + hi[1][0:rows_b])
    vals.append(hv[0][half:PEER_TOPK] + hv[1][0:1])
    eids.append(hi[0][half:PEER_TOPK] * PEER_NKEYS + hi[1][0:1])
    return jnp.concatenate(vals, axis=0), jnp.concatenate(eids, axis=0)


def _route_kernel(x_ref, oa_ref, ob_ref, wo_ref, g_ref, wq_ref, keys_ref, x1_ref, xn_ref, eid_ref, gate_ref):
    x1 = (x_ref[...] + jnp.dot(oa_ref[...], wo_ref[0:512, :], preferred_element_type=F32)
          + jnp.dot(ob_ref[...], wo_ref[512:1024, :], preferred_element_type=F32))
    x1_ref[...] = x1
    xn = x1 * lax.rsqrt(jnp.mean(x1 * x1, axis=-1, keepdims=True) + NORM_EPS) * g_ref[...]
    xn_ref[...] = xn
    q = jnp.dot(xn.astype(BF16), wq_ref[...], preferred_element_type=F32).astype(BF16)
    for h in range(PEER_HEADS):
        hv, hi = [], []
        for p in range(2):
            c0 = (h * 2 + p) * LANES
            sc = _nt(keys_ref[p], q[:, c0:c0 + LANES])
            v, i = _topk_rows(sc, PEER_TOPK)
            hv.append(v)
            hi.append(i)
        comb, cand_eid = _product_key_candidates(hv, hi)
        top, eid = _topk_rows(comb, PEER_TOPK, payload=cand_eid)
        e = jnp.exp(top - jnp.max(top, axis=0, keepdims=True))
        eid_ref[h * PEER_TOPK:(h + 1) * PEER_TOPK, :] = eid
        gate_ref[h * PEER_TOPK:(h + 1) * PEER_TOPK, :] = e / jnp.sum(e, axis=0, keepdims=True)


def _route(x, oa, ob, w_out, norm_ffn, peer_wq, peer_subkeys):
    t = x.shape[0]
    tm = 256
    assert t % tm == 0
    nt = t // tm
    row = lambda n: pl.BlockSpec((tm, n), lambda i: (i, 0))
    const = lambda shp: pl.BlockSpec(shp, lambda i: (0,) * len(shp))
    tr = pl.BlockSpec((None, PEER_HEADS * PEER_TOPK, tm), lambda i: (i, 0, 0))
    x1, xn, eid_t, gate_t = pl.pallas_call(
        _route_kernel,
        grid=(nt,),
        in_specs=[row(D_MODEL), row(512), row(512), const((D_MODEL, D_MODEL)), const((1, D_MODEL)),
                  const((D_MODEL, PEER_HEADS * PEER_DK)), const((2, PEER_NKEYS, PEER_DK // 2))],
        out_specs=(row(D_MODEL), row(D_MODEL), tr, tr),
        out_shape=(jax.ShapeDtypeStruct((t, D_MODEL), F32), jax.ShapeDtypeStruct((t, D_MODEL), F32),
                   jax.ShapeDtypeStruct((nt, 128, tm), I32), jax.ShapeDtypeStruct((nt, 128, tm), F32)),
        compiler_params=_params(("parallel",)),
        name="out_proj_route",
    )(x, oa, ob, w_out.astype(BF16), norm_ffn.reshape(1, D_MODEL), peer_wq.astype(BF16), peer_subkeys.astype(BF16))
    eid = eid_t.transpose(0, 2, 1).reshape(t, 128)
    gate = gate_t.transpose(0, 2, 1).reshape(t, 128)
    return x1, xn, eid, gate


_N_PAIRS = PEER_HEADS * PEER_TOPK
_TILE_ROWS = 16


_KROWS = _N_PAIRS * _TILE_ROWS


def _pack_table(tab):
    e = tab.shape[0]
    t = tab.astype(BF16).reshape(e // 2, 2, 8, LANES).transpose(0, 2, 3, 1)
    return lax.bitcast_convert_type(t, jnp.uint32).reshape(e * 4, LANES)


def _tile(tab_ref, r8):
    return pltpu.bitcast(tab_ref[pl.ds(pl.multiple_of(r8, 8), 8), :], BF16)


def _gather_pair(tab_ref, row_ref, ta, tb_):
    ga = jnp.concatenate([_tile(tab_ref, row_ref[ta, j]) for j in range(_N_PAIRS)], axis=0)
    gb = jnp.concatenate([_tile(tab_ref, row_ref[tb_, j]) for j in range(_N_PAIRS)], axis=0)
    return jnp.concatenate([ga, gb], axis=1)


def _diag_mask():
    return ((lax.broadcasted_iota(I32, (8, _KROWS), 1) % _TILE_ROWS) // 2) == lax.broadcasted_iota(I32, (8, _KROWS), 0)


def _peer_hidden_kernel(row_ref, x_ref, par_ref, gate_ref, sum_ref, u_ref, o_ref, d_ref, *, tb):
    diag = _diag_mask()
    zero = jnp.zeros((8, LANES), BF16)

    def pair(i, carry):
        ta = 2 * i
        xa = x_ref[ta].astype(BF16)
        xb = x_ref[ta + 1].astype(BF16)
        lhs = jnp.concatenate([jnp.concatenate([xa, zero], axis=1), jnp.concatenate([zero, xb], axis=1)], axis=0)
        res = _nt(lhs, _gather_pair(u_ref, row_ref, ta, ta + 1))
        d_ref[pl.ds(ta, 1), :] = jnp.sum(jnp.where(diag, res[0:8], 0.0), axis=0, keepdims=True)
        d_ref[pl.ds(ta + 1, 1), :] = jnp.sum(jnp.where(diag, res[8:16], 0.0), axis=0, keepdims=True)
        return carry

    lax.fori_loop(0, tb // 2, pair, 0, unroll=4)
    d = d_ref[...]
    hi = d.astype(BF16)
    lo = (d - hi.astype(F32)).astype(BF16)
    hl = (jnp.dot(hi, sum_ref[...], preferred_element_type=F32)
          + jnp.dot(lo, sum_ref[...], preferred_element_type=F32))
    hdn = jnp.where(par_ref[...] == 1, hl[:, _N_PAIRS:], hl[:, :_N_PAIRS])
    o_ref[...] = gate_ref[...] * _gelu(hdn)


def _peer_out_kernel(row_ref, par_ref, w_ref, x1_ref, gain_ref, exlo_ref, exhi_ref, v_ref, o_ref, wl_ref, *, tb):
    par = par_ref[...].astype(F32)
    w = w_ref[...]
    wl_ref[...] = (jnp.dot((w * (1.0 - par)).astype(BF16), exlo_ref[...], preferred_element_type=F32)
                   + jnp.dot((w * par).astype(BF16), exhi_ref[...], preferred_element_type=F32))
    diag = _diag_mask()

    def finish(t, acc):
        x2 = x1_ref[t] + acc
        ss = jnp.sum(jnp.sum(x2 * x2, axis=1, keepdims=True), axis=0, keepdims=True)
        o_ref[t] = x2 * lax.rsqrt(ss * (1.0 / D_MODEL) + NORM_EPS) * gain_ref[...]

    def pair(i, carry):
        ta = 2 * i
        lhs = jnp.concatenate(
            [jnp.where(diag, jnp.broadcast_to(wl_ref[pl.ds(t, 1), :], (8, _KROWS)), 0.0) for t in (ta, ta + 1)],
            axis=0).astype(BF16)
        res = jnp.dot(lhs, _gather_pair(v_ref, row_ref, ta, ta + 1), preferred_element_type=F32)
        finish(ta, res[0:8, 0:LANES])
        finish(ta + 1, res[8:16, LANES:2 * LANES])
        return carry

    lax.fori_loop(0, tb // 2, pair, 0, unroll=4)


def _peer_consts():
    col = np.arange(_KROWS)
    own = (col[None, :] // _TILE_ROWS) == np.arange(_N_PAIRS)[:, None]
    half = col[None, :] % 2
    exlo = (own & (half == 0)).astype(np.float32)
    exhi = (own & (half == 1)).astype(np.float32)
    summat = np.concatenate([exlo.T, exhi.T], axis=1)
    return jnp.asarray(exlo, BF16), jnp.asarray(exhi, BF16), jnp.asarray(summat, BF16)


def _peer_experts(x1, xn, eid, gate, u_tab, v_tab, norm_final):
    t = x1.shape[0]
    tb = 128
    assert t % tb == 0
    smem = pl.BlockSpec((tb, _N_PAIRS), lambda i: (i, 0), memory_space=pltpu.SMEM)
    vrow = pl.BlockSpec((tb, _N_PAIRS), lambda i: (i, 0))
    tok = pl.BlockSpec((tb, 8, LANES), lambda i: (i, 0, 0))
    table = pl.BlockSpec(memory_space=pltpu.VMEM)
    const = lambda shp: pl.BlockSpec(shp, lambda i: (0,) * len(shp))
    row = (eid >> 1) * 8
    par = eid & 1
    exlo, exhi, summat = _peer_consts()
    w = pl.pallas_call(
        functools.partial(_peer_hidden_kernel, tb=tb),
        grid=(t // tb,),
        in_specs=[smem, tok, vrow, vrow, const(summat.shape), table],
        out_specs=vrow,
        out_shape=jax.ShapeDtypeStruct((t, _N_PAIRS), F32),
        scratch_shapes=[pltpu.VMEM((tb, _KROWS), F32)],
        compiler_params=_params(("arbitrary",)),
        name="peer_hidden",
    )(row, xn.reshape(t, 8, LANES), par, gate, summat, u_tab)
    y = pl.pallas_call(
        functools.partial(_peer_out_kernel, tb=tb),
        grid=(t // tb,),
        in_specs=[smem, vrow, vrow, tok, const((8, LANES)), const(exlo.shape), const(exhi.shape), table],
        out_specs=tok,
        out_shape=jax.ShapeDtypeStruct((t, 8, LANES), F32),
        scratch_shapes=[pltpu.VMEM((tb, _KROWS), F32)],
        compiler_params=_params(("arbitrary",)),
        name="peer_out",
    )(row, par, w, x1.reshape(t, 8, LANES), norm_final.reshape(8, LANES), exlo, exhi, v_tab)
    return y.reshape(t, D_MODEL)


_PAGES_PER_STEP = 8


def _head_rows(n_rows, per_head, values_ref, base):
    rowi = lax.broadcasted_iota(I32, (n_rows, 1), 0)
    col = jnp.zeros((n_rows, 1), F32)
    for h in range(n_rows // per_head):
        col = jnp.where(rowi // per_head == h, values_ref[base + h], col)
    return col


def _paged_fetch(pt_ref, sem, make_copies, pages_per_step):
    b, st = pl.program_id(0), pl.program_id(1)
    n_st = pl.num_programs(1)
    step = b * n_st + st
    slot = step % 2

    @pl.when(step == 0)
    def _():
        for cp in make_copies(b, st * pages_per_step, slot):
            cp.start()

    @pl.when(step + 1 < pl.num_programs(0) * n_st)
    def _():
        nxt = step + 1
        for cp in make_copies(nxt // n_st, (nxt % n_st) * pages_per_step, 1 - slot):
            cp.start()

    for cp in make_copies(b, st * pages_per_step, slot):
        cp.wait()
    del sem
    return slot


def _diff_sample_kernel(pt_ref, lam_ref, slopes_ref, cache_ref, wq_ref, new_ref, subln_ref, o_ref,
                        m_ref, l_ref, acc_ref, buf_ref, sem, *, nq, past, page, lam_init):
    def copies(b, p0, slot):
        return [pltpu.make_async_copy(cache_ref.at[0, pt_ref[b, p0 + i], :, h, :],
                                      buf_ref.at[slot, h, pl.ds(i * page, page), :], sem.at[slot])
                for i in range(_PAGES_PER_STEP) for h in range(A_HEADS)]

    slot = _paged_fetch(pt_ref, sem, copies, _PAGES_PER_STEP)
    st = pl.program_id(1)
    hr = 2 * nq
    n_rows = A_HEADS * hr
    slope_col = _head_rows(n_rows, hr, slopes_ref, 0)
    rowi = lax.broadcasted_iota(I32, (n_rows, 1), 0)
    qpos_i = past + rowi % nq
    qpos = qpos_i.astype(F32)
    wq = wq_ref[...]

    @pl.when(st == 0)
    def _():
        m_ref[...] = jnp.full_like(m_ref, NEG)
        l_ref[...] = jnp.zeros_like(l_ref)
        acc_ref[...] = jnp.zeros_like(acc_ref)

    def update(ks, vs, k0, n, mask):
        kpos_i = k0 + lax.broadcasted_iota(I32, (1, n), 1)
        s = jnp.concatenate([_nt(wq[h * hr:(h + 1) * hr], ks[h]) for h in range(A_HEADS)], axis=0)
        s = s - slope_col * (qpos - kpos_i.astype(F32))
        if mask is not None:
            s = jnp.where(mask(kpos_i), s, NEG)
        m = m_ref[...]
        mn = jnp.maximum(m, jnp.max(s, axis=1, keepdims=True))
        c = jnp.exp(m - mn)
        p = jnp.exp(s - mn)
        pb = p.astype(BF16)
        pv = jnp.concatenate([jnp.dot(pb[h * hr:(h + 1) * hr], vs[h], preferred_element_type=F32)
                              for h in range(A_HEADS)], axis=0)
        m_ref[...] = mn
        l_ref[...] = l_ref[...] * c + jnp.sum(p, axis=1, keepdims=True)
        acc_ref[...] = acc_ref[...] * c + pv

    update([buf_ref[slot, h, :, 0:LANES].astype(BF16) for h in range(A_HEADS)],
           [buf_ref[slot, h, :, LANES:2 * LANES].astype(BF16) for h in range(A_HEADS)],
           st * _PAGES_PER_STEP * page, _PAGES_PER_STEP * page, None)

    @pl.when(st == pl.num_programs(1) - 1)
    def _():
        new = jnp.concatenate([new_ref[...], jnp.zeros((LANES - nq, new_ref.shape[1]), BF16)], axis=0)
        update([new[:, 2 * h * LANES:(2 * h + 1) * LANES] for h in range(A_HEADS)],
               [new[:, (2 * h + 1) * LANES:(2 * h + 2) * LANES] for h in range(A_HEADS)],
               past, LANES, lambda kp: kp <= qpos_i)
        o = acc_ref[...] / l_ref[...]
        lam = lam_ref[0]
        for h in range(A_HEADS):
            oh = o[h * hr:h * hr + nq] - lam * o[h * hr + nq:(h + 1) * hr]
            y = oh * lax.rsqrt(jnp.mean(oh * oh, axis=-1, keepdims=True) + NORM_EPS) * subln_ref[...]
            o_ref[:, h * A_VD:(h + 1) * A_VD] = (y * (1.0 - lam_init)).astype(BF16)


def _diff_attn_sample(aq, akvb, cache_a, page_table, lam, subln, db, nq, lam_init):
    n_phys, page = cache_a.shape[1], cache_a.shape[2]
    n_pages = page_table.shape[1]
    past = n_pages * page
    assert n_pages % _PAGES_PER_STEP == 0 and nq == 8
    n_rows = A_HEADS * 2 * nq
    q5 = aq.reshape(db, nq, A_HEADS, 2, A_DH).transpose(0, 2, 3, 1, 4)
    wq = jnp.einsum('bhcqd,cC->bhcqCd', q5, jnp.eye(2, dtype=BF16)).reshape(db, n_rows, 2 * A_DH)
    per_b = lambda r, c: pl.BlockSpec((None, r, c), lambda b, st, pt: (b, 0, 0))
    grid_spec = pltpu.PrefetchScalarGridSpec(
        num_scalar_prefetch=1,
        grid=(db, n_pages // _PAGES_PER_STEP),
        in_specs=[pl.BlockSpec(memory_space=pltpu.SMEM), pl.BlockSpec(memory_space=pltpu.SMEM),
                  pl.BlockSpec(memory_space=pl.ANY),
                  per_b(n_rows, 2 * A_DH), per_b(nq, 1024), pl.BlockSpec((1, A_VD), lambda b, st, pt: (0, 0))],
        out_specs=per_b(nq, 512),
        scratch_shapes=[pltpu.VMEM((n_rows, 1), F32), pltpu.VMEM((n_rows, 1), F32), pltpu.VMEM((n_rows, A_VD), F32),
                        pltpu.VMEM((2, A_HEADS, _PAGES_PER_STEP * page, 2 * LANES), F32),
                        pltpu.SemaphoreType.DMA((2,))],
    )
    out = pl.pallas_call(
        functools.partial(_diff_sample_kernel, nq=nq, past=past, page=page, lam_init=lam_init),
        grid_spec=grid_spec,
        out_shape=jax.ShapeDtypeStruct((db, nq, 512), BF16),
        compiler_params=_params(("arbitrary", "arbitrary")),
        name="diff_attn_sample",
    )(page_table, lam.reshape(1), jnp.asarray(_alibi(A_HEADS)), cache_a, wq,
      akvb.reshape(db, nq, 1024), subln.reshape(1, A_VD))
    return out.reshape(db * nq, 512)


def _compress_sample(cache_cmp, page_table, wbig, w2big, pec, db):
    n_phys, page = cache_cmp.shape[1], cache_cmp.shape[2]
    n_pages = page_table.shape[1]
    pps = math.gcd(n_pages, 32)
    assert page % (8 * CMP_STRIDE) == 0
    m = pps * page // CMP_STRIDE
    n_ch = n_pages * page // CMP_STRIDE
    del n_phys
    const = lambda shp: pl.BlockSpec(shp, lambda b, st, pt: (0, 0))
    grid_spec = pltpu.PrefetchScalarGridSpec(
        num_scalar_prefetch=1,
        grid=(db, n_pages // pps),
        in_specs=[pl.BlockSpec(memory_space=pl.ANY), const(wbig.shape), const(pec.shape), const(w2big.shape)],
        out_specs=pl.BlockSpec((None, m, 256), lambda b, st, pt: (b, st, 0)),
        scratch_shapes=[pltpu.VMEM((8, 512), F32), pltpu.VMEM((2, B_KV * pps, page, LANES), F32),
                        pltpu.SemaphoreType.DMA((2,))],
    )
    return pl.pallas_call(
        functools.partial(_compress_sample_kernel, n_in=pps, rows=page),
        grid_spec=grid_spec,
        out_shape=jax.ShapeDtypeStruct((db, n_ch, 256), BF16),
        compiler_params=_params(("arbitrary", "arbitrary")),
        name="nsa_compress_sample",
    )(page_table, cache_cmp, wbig, pec, w2big)


def _nsa_sample_kernel(pt_ref, slopes_ref, cache_ref, wq_ref, ckv_ref, win_ref, snew_ref, wnew_ref, gates_ref,
                       ovt_ref, onorm_ref, o_ref, neg_ref, oc_ref, m_ref, l_ref, acc_ref, buf_ref, sem,
                       *, nq, past, page, n_slc):
    def copies(b, p0, slot):
        return [pltpu.make_async_copy(cache_ref.at[0, pt_ref[b, p0 + i], :, g, :],
                                      buf_ref.at[slot, pl.ds(i * page, page), pl.ds(g * LANES, LANES)], sem.at[slot])
                for i in range(_PAGES_PER_STEP) for g in range(B_KV)]

    slot = _paged_fetch(pt_ref, sem, copies, _PAGES_PER_STEP)
    st = pl.program_id(1)
    n_rows = B_HEADS * nq
    n_j = neg_ref.shape[1]
    slope_col = _head_rows(n_rows, nq, slopes_ref, 0)
    rowi = lax.broadcasted_iota(I32, (n_rows, 1), 0)
    qpos_i = past + rowi % nq
    qpos = qpos_i.astype(F32)
    wq = wq_ref[...]
    init = (jnp.full((n_rows, 1), NEG, F32), jnp.zeros((n_rows, 1), F32), jnp.zeros((n_rows, 256), F32))

    def pad_rows(x):
        return jnp.concatenate([x, jnp.zeros((LANES - x.shape[0], x.shape[1]), x.dtype)], axis=0)

    def scores(rows_bf, kpos_i):
        return _nt(wq, rows_bf) - slope_col * (qpos - kpos_i.astype(F32))

    def block_bias(blk_of_key):
        n = blk_of_key.shape[1]
        onehot = (lax.broadcasted_iota(I32, (n_j, n), 0) == blk_of_key).astype(BF16)
        return jnp.dot(neg_ref[...], onehot, preferred_element_type=F32)

    @pl.when(st == 0)
    def _():
        ckv = ckv_ref[...]
        n_ch = ckv.shape[0]
        mcol = lax.broadcasted_iota(I32, (1, n_ch), 1)
        cpos = mcol * CMP_STRIDE + (CMP_STRIDE - 1)
        valid = (mcol >= 1) & (cpos <= qpos_i)
        s = jnp.where(valid, scores(ckv, cpos), NEG)
        p = jnp.exp(s - jnp.max(s, axis=1, keepdims=True))
        p = jnp.where(qpos_i >= CMP_LEN - 1, p / jnp.sum(p, axis=1, keepdims=True), 0.0)
        oc_ref[...] = jnp.dot(p.astype(BF16), ckv, preferred_element_type=F32)
        grp = B_REP * nq
        psum = jnp.concatenate(
            [sum(p[g * grp + r * nq:g * grp + (r + 1) * nq] for r in range(B_REP)) for g in range(B_KV)], axis=0)
        hi = psum.astype(BF16)
        lo = (psum - hi.astype(F32)).astype(BF16)
        imp_t = _nt(ovt_ref[...], hi) + _nt(ovt_ref[...], lo)
        shp = imp_t.shape
        jrow = lax.broadcasted_iota(I32, shp, 0)
        jcur = (past + lax.broadcasted_iota(I32, shp, 1) % nq) // SEL_BLOCK
        forced = (jrow == 0) | (jrow == jcur) | (jrow == jcur - 1)
        score = jnp.where(jrow > jcur, -FORCED, jnp.where(forced, FORCED, imp_t))
        rank = _rank_rows(score, n_slc)
        neg_t = jnp.where(rank < N_SEL, 0.0, NEG)
        neg = jnp.concatenate([neg_t, jnp.zeros((n_j, LANES - shp[1]), F32)], axis=1).T
        neg = jnp.concatenate([neg[g * nq:(g + 1) * nq] for g in range(B_KV) for _ in range(B_REP)], axis=0)
        neg_ref[...] = neg.astype(BF16)
        m_ref[...], l_ref[...], acc_ref[...] = init

    rows = buf_ref[slot].astype(BF16)
    n_keys = _PAGES_PER_STEP * page
    kpos_i = st * n_keys + lax.broadcasted_iota(I32, (1, n_keys), 1)
    m_ref[...], l_ref[...], acc_ref[...] = _online_update(
        (m_ref[...], l_ref[...], acc_ref[...]), scores(rows, kpos_i) + block_bias(kpos_i // SEL_BLOCK), rows)

    @pl.when(st == pl.num_programs(1) - 1)
    def _():
        npos_i = past + lax.broadcasted_iota(I32, (1, LANES), 1)
        causal = npos_i <= qpos_i
        snew = pad_rows(snew_ref[...].astype(BF16))
        s = jnp.where(causal, scores(snew, npos_i) + block_bias(npos_i // SEL_BLOCK), NEG)
        _, l_s, a_s = _online_update((m_ref[...], l_ref[...], acc_ref[...]), s, snew)
        win = win_ref[...].astype(BF16)
        wb = win.shape[0]
        wpos_i = past - wb + lax.broadcasted_iota(I32, (1, wb), 1)
        d = qpos_i - wpos_i
        cw = _online_update(init, jnp.where((d >= 0) & (d < WINDOW), scores(win, wpos_i), NEG), win)
        wnew = pad_rows(wnew_ref[...].astype(BF16))
        _, l_w, a_w = _online_update(cw, jnp.where(causal, scores(wnew, npos_i), NEG), wnew)
        gates = jnp.concatenate([gates_ref[...]] * B_HEADS, axis=0)
        glane = lax.broadcasted_iota(I32, (n_rows, LANES), 1)

        def gate_col(branch):
            return jnp.sum(jnp.where(glane == branch * B_HEADS + rowi // nq, gates, 0.0), axis=1, keepdims=True)

        o = gate_col(0) * oc_ref[...] + gate_col(1) * (a_s / l_s) + gate_col(2) * (a_w / l_w)
        lane = lax.broadcasted_iota(I32, (nq, LANES), 1)
        outs = []
        for hd in range(B_HEADS):
            g = hd // B_REP
            blk = jnp.where(lane >= B_DH, o[hd * nq:(hd + 1) * nq, g * LANES:(g + 1) * LANES], 0.0)
            y = blk * lax.rsqrt(jnp.sum(blk * blk, axis=1, keepdims=True) * (1.0 / B_DH) + NORM_EPS) * onorm_ref[...]
            outs.append(y[:, B_DH:])
        o_ref[...] = jnp.concatenate(outs, axis=1).astype(BF16)


def _overlap_rows(n_rows, n_slc, n_j):
    m = np.arange(n_rows)[None, :]
    start = (m - 1) * CMP_STRIDE
    j = np.arange(n_j)[:, None]
    ov = (start < (j + 1) * SEL_BLOCK) & (start + CMP_LEN > j * SEL_BLOCK) & (m >= 1) & (j < n_slc)
    return jnp.asarray(ov.astype(np.float32), BF16)


def _nsa_sample(bq, ckv, skv, wkv, gates, cache_cmp, cache_sel, state_win, page_table, cmp_pe, cmp_w1, cmp_w2,
                out_norm, db, nq):
    n_phys, page = cache_sel.shape[1], cache_sel.shape[2]
    n_pages = page_table.shape[1]
    past = n_pages * page
    wb = state_win.shape[2]
    assert nq == 8 and nq < CMP_STRIDE and past % SEL_BLOCK == 0 and wb == WINDOW and page % SEL_BLOCK == 0
    n_ch = past // CMP_STRIDE
    n_slc = -(-(past + nq) // SEL_BLOCK)
    n_j = -(-n_slc // LANES) * LANES
    n_rows = B_HEADS * nq
    wbig, w2big, pec = _prep_compress(cmp_pe, cmp_w1, cmp_w2, v_first=False)
    ckvc = _compress_sample(cache_cmp, page_table, wbig, w2big, pec, db)
    q = bq[:, :, B_DH:].reshape(B_KV, B_REP, db, nq, B_DH)
    wq = jnp.einsum('grbqd,gG->bgrqGd', q, jnp.eye(B_KV, dtype=BF16))
    wq = jnp.pad(wq, ((0, 0),) * 5 + ((0, B_DH),)).reshape(db, n_rows, 256)
    onorm = jnp.pad(out_norm, (B_DH, 0)).reshape(1, LANES)
    del n_phys
    per_b = lambda r, c: pl.BlockSpec((None, r, c), lambda b, st, pt: (b, 0, 0))
    const = lambda shp: pl.BlockSpec(shp, lambda b, st, pt: (0, 0))
    grid_spec = pltpu.PrefetchScalarGridSpec(
        num_scalar_prefetch=1,
        grid=(db, n_pages // _PAGES_PER_STEP),
        in_specs=[pl.BlockSpec(memory_space=pltpu.SMEM), pl.BlockSpec(memory_space=pl.ANY),
                  per_b(n_rows, 256), per_b(n_ch, 256), per_b(wb, 256), per_b(nq, 256), per_b(nq, 256),
                  per_b(nq, LANES), const((n_j, n_ch)), const((1, LANES))],
        out_specs=per_b(nq, 512),
        scratch_shapes=[pltpu.VMEM((n_rows, n_j), BF16), pltpu.VMEM((n_rows, 256), F32), pltpu.VMEM((n_rows, 1), F32),
                        pltpu.VMEM((n_rows, 1), F32), pltpu.VMEM((n_rows, 256), F32),
                        pltpu.VMEM((2, _PAGES_PER_STEP * page, B_KV * LANES), F32), pltpu.SemaphoreType.DMA((2,))],
    )
    out = pl.pallas_call(
        functools.partial(_nsa_sample_kernel, nq=nq, past=past, page=page, n_slc=n_slc),
        grid_spec=grid_spec,
        out_shape=jax.ShapeDtypeStruct((db, nq, 512), BF16),
        compiler_params=_params(("arbitrary", "arbitrary")),
        name="nsa_sample",
    )(page_table, jnp.asarray(_alibi(B_HEADS)), cache_sel, wq, ckvc,
      state_win.reshape(db, wb, 256), skv.reshape(db, nq, 256), wkv.reshape(db, nq, 256), gates.reshape(db, nq, LANES),
      _overlap_rows(n_ch, n_slc, n_j), onorm)
    return out.reshape(db * nq, 512)


def kernel(x_prompt, x_sample, cache_a, cache_cmp, cache_sel, state_win, page_table, norm_mix, w_in, diff_lambda,
           diff_subln, cmp_pe, cmp_w1, cmp_w2, nsa_out_norm, w_out, norm_ffn, peer_wq, peer_subkeys, peer_u, peer_v,
           norm_final):
    b, s, _ = x_prompt.shape
    db, nq, _ = x_sample.shape
    lam_init = 0.8 - 0.6 * math.exp(-0.3 * 0)
    dl = diff_lambda[0]
    lam = jnp.exp(jnp.sum(dl[0] * dl[1])) - jnp.exp(jnp.sum(dl[2] * dl[3])) + lam_init
    w = _prep_w_in(w_in[0])
    wbig, w2big, pec = _prep_compress(cmp_pe[0], cmp_w1[0], cmp_w2[0])
    u_tab = _pack_table(peer_u[0])
    v_tab = _pack_table(peer_v[0])

    ts = db * nq
    xs = x_sample.reshape(ts, D_MODEL)
    saq, sakv, sakvb, sbq, sckv, sskv, swkv, _, _, sgates = _in_proj(xs, norm_mix[0], w)
    soa = _diff_attn_sample(saq, sakvb, cache_a, page_table, lam, diff_subln[0], db, nq, lam_init)
    sob = _nsa_sample(sbq, sckv, sskv, swkv, sgates, cache_cmp, cache_sel, state_win, page_table, cmp_pe[0],
                      cmp_w1[0], cmp_w2[0], nsa_out_norm[0], db, nq)
    sx1, sxn, seid, sgate = _route(xs, soa, sob, w_out[0], norm_ffn[0], peer_wq[0], peer_subkeys[0])
    y_sample = _peer_experts(sx1, sxn, seid, sgate, u_tab, v_tab, norm_final).reshape(db, nq, D_MODEL)
    new_win = jnp.concatenate([state_win[0][:, nq:], swkv.reshape(db, nq, B_KV, 128)], axis=1)

    xp = x_prompt.reshape(b * s, D_MODEL)
    aq, akv, akvb, bq, ckv, skv, wkv, skvb, wkvb, gates = _in_proj(xp, norm_mix[0], w)
    oa = _diff_attn_prompt(aq, akvb, lam, diff_subln[0], b, s, lam_init)
    ckvc = _compress_prompt(ckv, b, s, wbig, w2big, pec)
    ob = _nsa_prompt(bq, ckvc, skvb, wkvb, gates, nsa_out_norm[0], b, s)
    x1, xn, eid, gate = _route(xp, oa, ob, w_out[0], norm_ffn[0], peer_wq[0], peer_subkeys[0])
    y_prompt = _peer_experts(x1, xn, eid, gate, u_tab, v_tab, norm_final).reshape(b, s, D_MODEL)
    win = min(WINDOW, s)
    outs_p = (akv.reshape(1, b, s, A_HEADS, 256), ckv.reshape(1, b, s, B_KV, 128), skv.reshape(1, b, s, B_KV, 128),
              wkv.reshape(b, s, B_KV, 128)[None, :, s - win:])
    return (y_prompt, y_sample, outs_p[0], sakv.reshape(1, db, nq, A_HEADS, 256), outs_p[1],
            sckv.reshape(1, db, nq, B_KV, 128), outs_p[2], sskv.reshape(1, db, nq, B_KV, 128), outs_p[3],
            new_win[None])
```

```python
import functools
import math

import jax
import jax.numpy as jnp
import numpy as np
from jax import lax
from jax.experimental import pallas as pl
from jax.experimental.pallas import tpu as pltpu

F32 = jnp.float32
BF16 = jnp.bfloat16
I32 = jnp.int32

D_MODEL = 1024
A_HEADS = 4
A_DH = 64
A_VD = 128
B_HEADS = 8
B_DH = 64
B_KV = 2
B_REP = 4
CMP_STRIDE = 16
CMP_LEN = 32
CMP_HID = 128
SEL_BLOCK = 64
N_SEL = 16
WINDOW = 512
PEER_HEADS = 8
PEER_NKEYS = 128
PEER_DK = 256
PEER_TOPK = 16
NORM_EPS = 1e-6
NEG = -1e30
FORCED = 1e9

VMEM_LIMIT_V7X = 52 * 1024 * 1024
LANES = 128

_C_AQ = 0
_C_AKV = 512
_C_BQ = 1536
_C_CKV = 2560
_C_SKV = 2816
_C_WKV = 3072
_C_GATE = 3328
_C_END = 3456


def _nt(a, b):
    return lax.dot_general(a, b, (((1,), (1,)), ((), ())), preferred_element_type=F32)


def _gelu(x):
    return 0.5 * x * (1.0 + lax.erf(x * 0.7071067811865476))


def _params(sem):
    return pltpu.CompilerParams(dimension_semantics=sem, vmem_limit_bytes=VMEM_LIMIT_V7X)


def _alibi(n):
    return np.asarray(2.0 ** (-8.0 * np.arange(1, n + 1) / n), dtype=np.float32)


def _prep_w_in(w_in):
    d = w_in.shape[0]
    aq = w_in[:, 0:512] * 0.125
    ak = w_in[:, 512:1024].reshape(d, A_HEADS, 2 * A_DH)
    av = w_in[:, 1024:1536].reshape(d, A_HEADS, A_VD)
    akv = jnp.concatenate([ak, av], axis=-1).reshape(d, 1024)
    bq = w_in[:, 1536:2048].reshape(d, B_HEADS, B_DH) * 0.125
    bq = jnp.concatenate([jnp.zeros_like(bq), bq], axis=-1).reshape(d, 1024)
    ckv = w_in[:, 2048:2304]
    skv = w_in[:, 2304:2560]
    wkv = w_in[:, 2560:2816]
    gt = w_in[:, 2816:2840].reshape(d, B_HEADS, 3).transpose(0, 2, 1).reshape(d, 24)
    gt = jnp.pad(gt, ((0, 0), (0, LANES - 24)))
    return jnp.concatenate([aq, akv, bq, ckv, skv, wkv, gt], axis=1).astype(BF16)


def _inproj_kernel(x_ref, g_ref, w_ref, aq_ref, akv_ref, akvb_ref, bq_ref, ckv_ref, skv_ref, wkv_ref,
                   skvb_ref, wkvb_ref, gate_ref):
    x = x_ref[...]
    xn = x * lax.rsqrt(jnp.mean(x * x, axis=-1, keepdims=True) + NORM_EPS) * g_ref[...]
    xb = xn.astype(BF16)

    def seg(a, b):
        return jnp.dot(xb, w_ref[:, a:b], preferred_element_type=F32)

    aq_ref[...] = seg(_C_AQ, _C_AKV).astype(BF16)
    akv = seg(_C_AKV, _C_BQ)
    akv_ref[...] = akv
    akvb_ref[...] = akv.astype(BF16)
    bq = seg(_C_BQ, _C_CKV).astype(BF16)
    for hd in range(B_HEADS):
        bq_ref[hd] = bq[:, hd * LANES:(hd + 1) * LANES]
    ckv_ref[...] = seg(_C_CKV, _C_SKV)
    for (a, fref, bref) in ((_C_SKV, skv_ref, skvb_ref), (_C_WKV, wkv_ref, wkvb_ref)):
        kv = seg(a, a + 2 * LANES)
        fref[...] = kv
        sw = jnp.concatenate([pltpu.roll(kv[:, g * LANES:(g + 1) * LANES], 64, 1) for g in range(B_KV)], axis=1)
        bref[...] = sw.astype(BF16)
    gate_ref[...] = jax.nn.sigmoid(seg(_C_GATE, _C_END))


def _in_proj(x, gain, w):
    t = x.shape[0]
    tm = 256
    assert t % tm == 0
    row = lambda n: pl.BlockSpec((tm, n), lambda i: (i, 0))
    out_shape = (
        jax.ShapeDtypeStruct((t, 512), BF16),
        jax.ShapeDtypeStruct((t, 1024), F32),
        jax.ShapeDtypeStruct((t, 1024), BF16),
        jax.ShapeDtypeStruct((B_HEADS, t, LANES), BF16),
        jax.ShapeDtypeStruct((t, 256), F32),
        jax.ShapeDtypeStruct((t, 256), F32),
        jax.ShapeDtypeStruct((t, 256), F32),
        jax.ShapeDtypeStruct((t, 256), BF16),
        jax.ShapeDtypeStruct((t, 256), BF16),
        jax.ShapeDtypeStruct((t, LANES), F32),
    )
    out_specs = (row(512), row(1024), row(1024), pl.BlockSpec((B_HEADS, tm, LANES), lambda i: (0, i, 0)),
                 row(256), row(256), row(256), row(256), row(256), row(LANES))
    return pl.pallas_call(
        _inproj_kernel,
        grid=(t // tm,),
        in_specs=[row(D_MODEL), pl.BlockSpec((1, D_MODEL), lambda i: (0, 0)),
                  pl.BlockSpec((D_MODEL, _C_END), lambda i: (0, 0))],
        out_specs=out_specs,
        out_shape=out_shape,
        compiler_params=_params(("parallel",)),
        name="in_proj",
    )(x, gain.reshape(1, D_MODEL), w)


def _online_update(carry, s, v):
    m, l, acc = carry
    mn = jnp.maximum(m, jnp.max(s, axis=1, keepdims=True))
    c = jnp.exp(m - mn)
    p = jnp.exp(s - mn)
    l = l * c + jnp.sum(p, axis=1, keepdims=True)
    acc = acc * c + jnp.dot(p.astype(BF16), v, preferred_element_type=F32)
    return mn, l, acc


def _diff_prompt_kernel(lam_ref, slopes_ref, q_ref, kv_ref, subln_ref, o_ref, *, tq, tk, lam_init):
    qi = pl.program_id(1)
    q0 = qi * tq
    lam = lam_ref[0]
    lane = lax.broadcasted_iota(I32, (tq, LANES), 1)
    rel = (lax.broadcasted_iota(I32, (tq, tk), 0) - lax.broadcasted_iota(I32, (tq, tk), 1))
    relf = rel.astype(F32)
    qs = []
    for h in range(A_HEADS):
        q = q_ref[:, h * LANES:(h + 1) * LANES]
        zero = jnp.zeros_like(q)
        qs.append((jnp.where(lane < A_DH, q, zero), jnp.where(lane >= A_DH, q, zero)))

    def tile(kt, carry, diag):
        k0 = pl.multiple_of(kt * tk, tk)
        dist = relf + (q0 - k0).astype(F32)
        out = []
        for h in range(A_HEADS):
            kk = kv_ref[pl.ds(k0, tk), 2 * h * LANES:(2 * h + 1) * LANES]
            v = kv_ref[pl.ds(k0, tk), (2 * h + 1) * LANES:(2 * h + 2) * LANES]
            bias = slopes_ref[h] * dist
            for c in range(2):
                s = _nt(qs[h][c], kk) - bias
                if diag:
                    s = jnp.where(dist >= 0, s, NEG)
                out.append(_online_update(carry[2 * h + c], s, v))
        return tuple(out)

    init1 = (jnp.full((tq, 1), NEG, F32), jnp.zeros((tq, 1), F32), jnp.zeros((tq, A_VD), F32))
    n_full = q0 // tk
    carry = lax.fori_loop(0, n_full, lambda kt, c: tile(kt, c, False), (init1,) * (2 * A_HEADS))
    for d in range(max(tq // tk, 1)):
        carry = tile(n_full + d, carry, True)
    for h in range(A_HEADS):
        (_, l1, a1), (_, l2, a2) = carry[2 * h], carry[2 * h + 1]
        o = a1 / l1 - lam * (a2 / l2)
        y = o * lax.rsqrt(jnp.mean(o * o, axis=-1, keepdims=True) + NORM_EPS) * subln_ref[...]
        o_ref[:, h * LANES:(h + 1) * LANES] = (y * (1.0 - lam_init)).astype(BF16)


def _diff_attn_prompt(aq, akvb, lam, subln, b, s, lam_init):
    tq, tk = 256, 512
    assert s % tq == 0 and s % tk == 0 and (tq % tk == 0 or tk % tq == 0)
    aq = aq.reshape(b, s, 512)
    akvb = akvb.reshape(b, s, 1024)
    out = pl.pallas_call(
        functools.partial(_diff_prompt_kernel, tq=tq, tk=tk, lam_init=lam_init),
        grid=(b, s // tq),
        in_specs=[pl.BlockSpec(memory_space=pltpu.SMEM), pl.BlockSpec(memory_space=pltpu.SMEM),
                  pl.BlockSpec((None, tq, 512), lambda bi, qi: (bi, qi, 0)),
                  pl.BlockSpec((None, s, 1024), lambda bi, qi: (bi, 0, 0)),
                  pl.BlockSpec((1, A_VD), lambda bi, qi: (0, 0))],
        out_specs=pl.BlockSpec((None, tq, 512), lambda bi, qi: (bi, qi, 0)),
        out_shape=jax.ShapeDtypeStruct((b, s, 512), BF16),
        compiler_params=_params(("parallel", "arbitrary")),
        name="diff_attn_prompt",
    )(lam.reshape(1), jnp.asarray(_alibi(A_HEADS)), aq, akvb, subln.reshape(1, A_VD))
    return out.reshape(b * s, 512)


def _prep_compress(cmp_pe, cmp_w1, cmp_w2, v_first=True):
    w1r = cmp_w1.reshape(2, 2, CMP_STRIDE, B_DH, CMP_HID)
    base = w1r.transpose(2, 0, 3, 1, 4)
    wbig = jnp.zeros((CMP_STRIDE, B_KV, 2, B_DH, 2, B_KV, 2, CMP_HID), F32)
    w2big = jnp.zeros((B_KV, 2, CMP_HID, B_KV, 2, B_DH), F32)
    for g in range(B_KV):
        for e in range(2):
            wbig = wbig.at[:, g, e, :, :, g, e, :].set(base[:, e])
            w2big = w2big.at[g, e, :, g, (1 - e) if v_first else e, :].set(cmp_w2[e])
    wbig = wbig.reshape(CMP_STRIDE * 256, 1024).astype(BF16)
    w2big = w2big.reshape(512, 256).astype(BF16)
    per = cmp_pe.reshape(2, 2, CMP_STRIDE, B_DH)
    pec = jnp.broadcast_to(per.transpose(1, 2, 0, 3)[:, :, None], (2, CMP_STRIDE, B_KV, 2, B_DH))
    pec = jnp.pad(pec.reshape(2, CMP_STRIDE * 256), ((0, 6), (0, 0))).astype(BF16)
    return wbig, w2big, pec


def _compress_kernel(*refs, n_in, rows):
    x_refs = refs[-(2 * n_in + 5):-5]
    m_each = rows // CMP_STRIDE
    _compress_core(lambda i, g, s: x_refs[2 * i + g][pl.ds(s, m_each, stride=CMP_STRIDE), :], n_in, *refs[-5:])


def _compress_sample_kernel(pt_ref, cache_ref, w1_ref, pe_ref, w2_ref, o_ref, prev_ref, buf_ref, sem, *, n_in, rows):
    def copies(b, p0, slot):
        return [pltpu.make_async_copy(cache_ref.at[0, pt_ref[b, p0 + i], :, g, :], buf_ref.at[slot, 2 * i + g],
                                      sem.at[slot])
                for i in range(n_in) for g in range(B_KV)]

    slot = _paged_fetch(pt_ref, sem, copies, n_in)
    m_each = rows // CMP_STRIDE
    _compress_core(lambda i, g, s: buf_ref[slot, 2 * i + g, pl.ds(s, m_each, stride=CMP_STRIDE), :], n_in,
                   w1_ref, pe_ref, w2_ref, o_ref, prev_ref)


def _compress_core(chunk_rows, n_in, w1_ref, pe_ref, w2_ref, o_ref, prev_ref):
    step = pl.program_id(1)

    @pl.when(step == 0)
    def _():
        prev_ref[...] = jnp.zeros_like(prev_ref)

    pieces = []
    for i in range(n_in):
        cols = [chunk_rows(i, g, s).astype(BF16) for s in range(CMP_STRIDE) for g in range(B_KV)]
        pieces.append(jnp.concatenate(cols, axis=1))
    c = pieces[0] if n_in == 1 else jnp.concatenate(pieces, axis=0)
    m = c.shape[0]
    p = jnp.dot(c, w1_ref[...], preferred_element_type=F32)
    pp = jnp.dot(pe_ref[...], w1_ref[...], preferred_element_type=F32)
    pe_term = pp[0:1, :512] + pp[1:2, 512:]
    p0 = p[:, :512]
    p1 = p[:, 512:]
    row = lax.broadcasted_iota(I32, (m, 512), 0)
    p0s = jnp.where(row == 0, prev_ref[7:8, :], pltpu.roll(p0, 1, 0))
    prev_ref[...] = p0[m - 8:m, :]
    act = _gelu(p0s + p1 + pe_term)
    o_ref[...] = jnp.dot(act.astype(BF16), w2_ref[...], preferred_element_type=F32).astype(BF16)


def _compress_prompt(ckv, b, s, wbig, w2big, pec):
    rows = min(s, 2048)
    assert s % rows == 0 and rows % (8 * CMP_STRIDE) == 0
    n_ch = s // CMP_STRIDE
    m = rows // CMP_STRIDE
    const = lambda shp: pl.BlockSpec(shp, lambda bi, st: (0, 0))
    return pl.pallas_call(
        functools.partial(_compress_kernel, n_in=1, rows=rows),
        grid=(b, s // rows),
        in_specs=[pl.BlockSpec((None, rows, LANES), lambda bi, st: (bi, st, 0)),
                  pl.BlockSpec((None, rows, LANES), lambda bi, st: (bi, st, 1)),
                  const(wbig.shape), const(pec.shape), const(w2big.shape)],
        out_specs=pl.BlockSpec((None, m, 256), lambda bi, st: (bi, st, 0)),
        out_shape=jax.ShapeDtypeStruct((b, n_ch, 256), BF16),
        scratch_shapes=[pltpu.VMEM((8, 512), F32)],
        compiler_params=_params(("parallel", "arbitrary")),
        name="nsa_compress_prompt",
    )(ckv.reshape(b, s, 256), ckv.reshape(b, s, 256), wbig, pec, w2big)


def _rank_rows(score, n_rows):
    jrow = lax.broadcasted_iota(I32, score.shape, 0)
    rank = jnp.zeros(score.shape, I32)
    for j2 in range(n_rows):
        sj = score[j2:j2 + 1, :]
        rank = rank + jnp.where(jrow > j2, (sj >= score).astype(I32), (sj > score).astype(I32))
    return rank


def _nsa_prompt_kernel(slopes_ref, q_ref, ckv_ref, ske_ref, skv_ref, wkv_ref, gates_ref, ovt_ref, onorm_ref,
                       o_ref, *, tq, tk, n_slc):
    qi = pl.program_id(1)
    q0 = qi * tq
    mrows = B_REP * tq
    n_ch = ckv_ref.shape[0]
    rowi = lax.broadcasted_iota(I32, (mrows, 1), 0)
    qpos_i = q0 + rowi % tq
    qpos = qpos_i.astype(F32)
    lane = lax.broadcasted_iota(I32, (mrows, LANES), 1)
    n_jr = -(-n_slc // 8) * 8
    jrow = lax.broadcasted_iota(I32, (n_jr, tq), 0)
    jcur = (q0 + lax.broadcasted_iota(I32, (n_jr, tq), 1)) // SEL_BLOCK
    forced = (jrow == 0) | (jrow == jcur) | (jrow == jcur - 1)
    mcol = lax.broadcasted_iota(I32, (1, n_ch), 1)
    cpos = jnp.where(mcol >= 1, mcol * CMP_STRIDE + (CMP_STRIDE - 1), 2 ** 30).astype(F32)

    def lanes_of(g):
        return slice(g * LANES, (g + 1) * LANES)

    def prepare(g):
        q = q_ref[g * B_REP:(g + 1) * B_REP].reshape(mrows, LANES)
        slope_col = jnp.zeros((mrows, 1), F32)
        for r in range(B_REP):
            slope_col = jnp.where(rowi // tq == r, slopes_ref[g * B_REP + r], slope_col)
        ckv = ckv_ref[:, lanes_of(g)]
        s = _nt(q, ckv)
        s = jnp.where(cpos <= qpos, s - slope_col * (qpos - cpos), NEG)
        p = jnp.exp(s - jnp.max(s, axis=1, keepdims=True))
        p = jnp.where(qpos_i >= CMP_LEN - 1, p / jnp.sum(p, axis=1, keepdims=True), 0.0)
        o_c = jnp.dot(p.astype(BF16), ckv, preferred_element_type=F32)
        psum = p[0:tq] + p[tq:2 * tq] + p[2 * tq:3 * tq] + p[3 * tq:4 * tq]
        hi = psum.astype(BF16)
        lo = (psum - hi.astype(F32)).astype(BF16)
        imp_t = _nt(ovt_ref[...], hi) + _nt(ovt_ref[...], lo)
        score = jnp.where(jrow > jcur, -FORCED, jnp.where(forced, FORCED, imp_t[0:n_jr]))
        neg = jnp.where(_rank_rows(score, n_slc) < N_SEL, 0.0, NEG)
        neg = jnp.concatenate([neg, jnp.zeros((LANES - n_jr, tq), F32)], axis=0).T
        neg4 = jnp.concatenate([neg] * B_REP, axis=0).astype(BF16)
        q_aug = jnp.where(lane < B_DH, neg4, q)
        return q, q_aug, slope_col, o_c

    grp = [prepare(g) for g in range(B_KV)]

    def attend(g, carry, qq, kk, vv, k0, n, mask_fn):
        kpos_i = k0 + lax.broadcasted_iota(I32, (1, n), 1)
        sc = _nt(qq, kk) - grp[g][2] * (qpos - kpos_i.astype(F32))
        if mask_fn is not None:
            sc = jnp.where(mask_fn(kpos_i), sc, NEG)
        return _online_update(carry, sc, vv)

    init = (jnp.full((mrows, 1), NEG, F32), jnp.zeros((mrows, 1), F32), jnp.zeros((mrows, LANES), F32))

    def sel_tile(kt, carries, diag):
        k0 = pl.multiple_of(kt * tk, tk)
        return tuple(attend(g, carries[g], grp[g][1], ske_ref[pl.ds(k0, tk), lanes_of(g)],
                            skv_ref[pl.ds(k0, tk), lanes_of(g)], k0, tk, (lambda kp: kp <= qpos_i) if diag else None)
                     for g in range(B_KV))

    n_full = q0 // tk
    carries = lax.fori_loop(0, n_full, lambda kt, c: sel_tile(kt, c, False), (init,) * B_KV)
    carries = sel_tile(n_full, carries, True)

    wlen = min(WINDOW + tq, wkv_ref.shape[0])
    k0w = pl.multiple_of(jnp.maximum(q0 + tq - wlen, 0), tq)

    def wmask(kp):
        return (qpos_i - kp).astype(jnp.uint32) < WINDOW

    gates = gates_ref[...]
    glane = lax.broadcasted_iota(I32, (tq, LANES), 1)
    for g in range(B_KV):
        q, _, _, o_c = grp[g]
        _, l_s, a_s = carries[g]
        kvw = wkv_ref[pl.ds(k0w, wlen), lanes_of(g)]
        _, l_w, a_w = attend(g, init, q, kvw, kvw, k0w, wlen, wmask)

        def gate_col(branch, g=g):
            cols = [jnp.sum(jnp.where(glane == branch * B_HEADS + g * B_REP + r, gates, 0.0), axis=1, keepdims=True)
                    for r in range(B_REP)]
            return jnp.concatenate(cols, axis=0)

        o = gate_col(0) * o_c + gate_col(1) * (a_s / l_s) + gate_col(2) * (a_w / l_w)
        o = jnp.where(lane < B_DH, o, 0.0)
        y = o * lax.rsqrt(jnp.sum(o * o, axis=1, keepdims=True) * (1.0 / B_DH) + NORM_EPS) * onorm_ref[...]
        o_ref[g * B_REP:(g + 1) * B_REP] = y.reshape(B_REP, tq, LANES)[:, :, :B_DH].astype(BF16)


def _overlap_t(n_rows, n_slc):
    m = np.arange(n_rows)[None, :]
    start = (m - 1) * CMP_STRIDE
    j = np.arange(LANES)[:, None]
    ov = (start < (j + 1) * SEL_BLOCK) & (start + CMP_LEN > j * SEL_BLOCK) & (m >= 1) & (j < n_slc)
    return jnp.asarray(ov.astype(np.float32), BF16)


def _nsa_prompt(bq, ckvc, skvb, wkvb, gates, out_norm, b, s):
    tq, tk = 128, 512
    tk = min(tk, s)
    assert s % tk == 0 and tk % tq == 0 and s >= WINDOW
    n_ch = s // CMP_STRIDE
    n_slc = s // SEL_BLOCK
    assert n_slc <= B_DH
    blk = np.arange(s) // SEL_BLOCK
    onehot = jnp.asarray((blk[:, None] == np.arange(LANES)[None, :]).astype(np.float32), BF16)
    skvb = skvb.reshape(b, s, 256)
    lane = jnp.arange(256) % LANES
    ske = jnp.where(lane < B_DH, jnp.concatenate([onehot, onehot], axis=1)[None], skvb)
    onorm = jnp.pad(out_norm, (0, LANES - B_DH)).reshape(1, LANES)
    per_b = lambda n: pl.BlockSpec((None, n, B_KV * LANES), lambda bi, qi: (bi, 0, 0))
    out = pl.pallas_call(
        functools.partial(_nsa_prompt_kernel, tq=tq, tk=tk, n_slc=n_slc),
        grid=(b, s // tq),
        in_specs=[pl.BlockSpec(memory_space=pltpu.SMEM),
                  pl.BlockSpec((B_HEADS, None, tq, LANES), lambda bi, qi: (0, bi, qi, 0)),
                  per_b(n_ch), per_b(s), per_b(s), per_b(s),
                  pl.BlockSpec((None, tq, LANES), lambda bi, qi: (bi, qi, 0)),
                  pl.BlockSpec((LANES, n_ch), lambda bi, qi: (0, 0)),
                  pl.BlockSpec((1, LANES), lambda bi, qi: (0, 0))],
        out_specs=pl.BlockSpec((B_HEADS, None, tq, B_DH), lambda bi, qi: (0, bi, qi, 0)),
        out_shape=jax.ShapeDtypeStruct((B_HEADS, b, s, B_DH), BF16),
        compiler_params=_params(("parallel", "arbitrary")),
        name="nsa_prompt",
    )(jnp.asarray(_alibi(B_HEADS)), bq.reshape(B_HEADS, b, s, LANES), ckvc, ske, skvb,
      wkvb.reshape(b, s, 256), gates.reshape(b, s, LANES), _overlap_t(n_ch, n_slc), onorm)
    return out.transpose(1, 2, 0, 3).reshape(b * s, 512)


def _topk_rows(x, k, payload=None):
    n = x.shape[0]
    rows = lax.broadcasted_iota(I32, x.shape, 0)
    vals, outs = [], []
    cur = x
    for _ in range(k):
        m = jnp.max(cur, axis=0, keepdims=True)
        idx = jnp.min(jnp.where(cur == m, rows, n), axis=0, keepdims=True)
        hit = rows == idx
        vals.append(m)
        outs.append(idx if payload is None else jnp.sum(jnp.where(hit, payload, 0), axis=0, keepdims=True))
        cur = jnp.where(hit, -jnp.inf, cur)
    return jnp.concatenate(vals, axis=0), jnp.concatenate(outs, axis=0)


def _product_key_candidates(hv, hi):
    half = PEER_TOPK // 2
    vals, eids = [], []
    for a in range(half):
        nb = PEER_TOPK // (a + 1)
        rows_b = PEER_TOPK if a == 0 else half
        v = hv[0][a:a + 1] + hv[1][0:rows_b]
        if nb < rows_b:
            v = jnp.where(lax.broadcasted_iota(I32, v.shape, 0) < nb, v, -jnp.inf)
        vals.append(v)
        eids.append(hi[0][a:a + 1] * PEER_NKEYS + hi[1][0:rows_b])
    vals.append(hv[0][half:PEER_TOPK] + hv[1][0:1])
    eids.append(hi[0][half:PEER_TOPK] * PEER_NKEYS + hi[1][0:1])
    return jnp.concatenate(vals, axis=0), jnp.concatenate(eids, axis=0)


def _route_kernel(x_ref, oa_ref, ob_ref, wo_ref, g_ref, wq_ref, keys_ref, x1_ref, xn_ref, eid_ref, gate_ref):
    x1 = (x_ref[...] + jnp.dot(oa_ref[...], wo_ref[0:512, :], preferred_element_type=F32)
          + jnp.dot(ob_ref[...], wo_ref[512:1024, :], preferred_element_type=F32))
    x1_ref[...] = x1
    xn = x1 * lax.rsqrt(jnp.mean(x1 * x1, axis=-1, keepdims=True) + NORM_EPS) * g_ref[...]
    xn_ref[...] = xn
    q = jnp.dot(xn.astype(BF16), wq_ref[...], preferred_element_type=F32).astype(BF16)
    for h in range(PEER_HEADS):
        hv, hi = [], []
        for p in range(2):
            c0 = (h * 2 + p) * LANES
            sc = _nt(keys_ref[p], q[:, c0:c0 + LANES])
            v, i = _topk_rows(sc, PEER_TOPK)
            hv.append(v)
            hi.append(i)
        comb, cand_eid = _product_key_candidates(hv, hi)
        top, eid = _topk_rows(comb, PEER_TOPK, payload=cand_eid)
        e = jnp.exp(top - jnp.max(top, axis=0, keepdims=True))
        eid_ref[h * PEER_TOPK:(h + 1) * PEER_TOPK, :] = eid
        gate_ref[h * PEER_TOPK:(h + 1) * PEER_TOPK, :] = e / jnp.sum(e, axis=0, keepdims=True)


def _route(x, oa, ob, w_out, norm_ffn, peer_wq, peer_subkeys):
    t = x.shape[0]
    tm = 256
    assert t % tm == 0
    nt = t // tm
    row = lambda n: pl.BlockSpec((tm, n), lambda i: (i, 0))
    const = lambda shp: pl.BlockSpec(shp, lambda i: (0,) * len(shp))
    tr = pl.BlockSpec((None, PEER_HEADS * PEER_TOPK, tm), lambda i: (i, 0, 0))
    x1, xn, eid_t, gate_t = pl.pallas_call(
        _route_kernel,
        grid=(nt,),
        in_specs=[row(D_MODEL), row(512), row(512), const((D_MODEL, D_MODEL)), const((1, D_MODEL)),
                  const((D_MODEL, PEER_HEADS * PEER_DK)), const((2, PEER_NKEYS, PEER_DK // 2))],
        out_specs=(row(D_MODEL), row(D_MODEL), tr, tr),
        out_shape=(jax.ShapeDtypeStruct((t, D_MODEL), F32), jax.ShapeDtypeStruct((t, D_MODEL), F32),
                   jax.ShapeDtypeStruct((nt, 128, tm), I32), jax.ShapeDtypeStruct((nt, 128, tm), F32)),
        compiler_params=_params(("parallel",)),
        name="out_proj_route",
    )(x, oa, ob, w_out.astype(BF16), norm_ffn.reshape(1, D_MODEL), peer_wq.astype(BF16), peer_subkeys.astype(BF16))
    eid = eid_t.transpose(0, 2, 1).reshape(t, 128)
    gate = gate_t.transpose(0, 2, 1).reshape(t, 128)
    return x1, xn, eid, gate


_N_PAIRS = PEER_HEADS * PEER_TOPK
_TILE_ROWS = 16


_KROWS = _N_PAIRS * _TILE_ROWS


def _pack_table(tab):
    e = tab.shape[0]
    t = tab.astype(BF16).reshape(e // 2, 2, 8, LANES).transpose(0, 2, 3, 1)
    return lax.bitcast_convert_type(t, jnp.uint32).reshape(e * 4, LANES)


def _tile(tab_ref, r8):
    return pltpu.bitcast(tab_ref[pl.ds(pl.multiple_of(r8, 8), 8), :], BF16)


def _gather_pair(tab_ref, row_ref, ta, tb_):
    ga = jnp.concatenate([_tile(tab_ref, row_ref[ta, j]) for j in range(_N_PAIRS)], axis=0)
    gb = jnp.concatenate([_tile(tab_ref, row_ref[tb_, j]) for j in range(_N_PAIRS)], axis=0)
    return jnp.concatenate([ga, gb], axis=1)


def _row_tile(ref, t):
    sub = lax.broadcasted_iota(I32, (8, LANES), 0)
    rowb = jnp.broadcast_to(ref[pl.ds(t, 1), :], (8, ref.shape[1]))
    tile = rowb[:, 0:LANES]
    for c in range(1, 8):
        tile = jnp.where(sub == c, rowb[:, c * LANES:(c + 1) * LANES], tile)
    return tile


def _diag_mask():
    return ((lax.broadcasted_iota(I32, (8, _KROWS), 1) % _TILE_ROWS) // 2) == lax.broadcasted_iota(I32, (8, _KROWS), 0)


def _peer_hidden_kernel(row_ref, x_ref, par_ref, gate_ref, sum_ref, u_ref, o_ref, d_ref, *, tb):
    diag = _diag_mask()
    zero = jnp.zeros((8, LANES), BF16)

    def pair(i, carry):
        ta = 2 * i
        xa = _row_tile(x_ref, ta).astype(BF16)
        xb = _row_tile(x_ref, ta + 1).astype(BF16)
        lhs = jnp.concatenate([jnp.concatenate([xa, zero], axis=1), jnp.concatenate([zero, xb], axis=1)], axis=0)
        res = _nt(lhs, _gather_pair(u_ref, row_ref, ta, ta + 1))
        d_ref[pl.ds(ta, 1), :] = jnp.sum(jnp.where(diag, res[0:8], 0.0), axis=0, keepdims=True)
        d_ref[pl.ds(ta + 1, 1), :] = jnp.sum(jnp.where(diag, res[8:16], 0.0), axis=0, keepdims=True)
        return carry

    lax.fori_loop(0, tb // 2, pair, 0, unroll=16)
    d = d_ref[...]
    hi = d.astype(BF16)
    lo = (d - hi.astype(F32)).astype(BF16)
    hl = (jnp.dot(hi, sum_ref[...], preferred_element_type=F32)
          + jnp.dot(lo, sum_ref[...], preferred_element_type=F32))
    hdn = jnp.where(par_ref[...] == 1, hl[:, _N_PAIRS:], hl[:, :_N_PAIRS])
    o_ref[...] = gate_ref[...] * _gelu(hdn)


def _peer_out_kernel(row_ref, par_ref, w_ref, x1_ref, gain_ref, exlo_ref, exhi_ref, v_ref, o_ref, wl_ref, *, tb):
    par = par_ref[...].astype(F32)
    w = w_ref[...]
    wl_ref[...] = (jnp.dot((w * (1.0 - par)).astype(BF16), exlo_ref[...], preferred_element_type=F32)
                   + jnp.dot((w * par).astype(BF16), exhi_ref[...], preferred_element_type=F32))
    diag = _diag_mask()

    def finish(t, acc):
        x2 = _row_tile(x1_ref, t) + acc
        ss = jnp.sum(jnp.sum(x2 * x2, axis=1, keepdims=True), axis=0, keepdims=True)
        o_ref[t] = x2 * lax.rsqrt(ss * (1.0 / D_MODEL) + NORM_EPS) * gain_ref[...]

    def pair(i, carry):
        ta = 2 * i
        lhs = jnp.concatenate(
            [jnp.where(diag, jnp.broadcast_to(wl_ref[pl.ds(t, 1), :], (8, _KROWS)), 0.0) for t in (ta, ta + 1)],
            axis=0).astype(BF16)
        res = jnp.dot(lhs, _gather_pair(v_ref, row_ref, ta, ta + 1), preferred_element_type=F32)
        finish(ta, res[0:8, 0:LANES])
        finish(ta + 1, res[8:16, LANES:2 * LANES])
        return carry

    lax.fori_loop(0, tb // 2, pair, 0, unroll=16)


def _peer_consts():
    col = np.arange(_KROWS)
    own = (col[None, :] // _TILE_ROWS) == np.arange(_N_PAIRS)[:, None]
    half = col[None, :] % 2
    exlo = (own & (half == 0)).astype(np.float32)
    exhi = (own & (half == 1)).astype(np.float32)
    summat = np.concatenate([exlo.T, exhi.T], axis=1)
    return jnp.asarray(exlo, BF16), jnp.asarray(exhi, BF16), jnp.asarray(summat, BF16)


def _peer_experts(x1, xn, eid, gate, u_tab, v_tab, norm_final):
    t = x1.shape[0]
    tb = 128
    assert t % tb == 0
    smem = pl.BlockSpec((tb, _N_PAIRS), lambda i: (i, 0), memory_space=pltpu.SMEM)
    vrow = pl.BlockSpec((tb, _N_PAIRS), lambda i: (i, 0))
    tok = pl.BlockSpec((tb, D_MODEL), lambda i: (i, 0))
    table = pl.BlockSpec(memory_space=pltpu.VMEM)
    const = lambda shp: pl.BlockSpec(shp, lambda i: (0,) * len(shp))
    row = (eid >> 1) * 8
    par = eid & 1
    exlo, exhi, summat = _peer_consts()
    w = pl.pallas_call(
        functools.partial(_peer_hidden_kernel, tb=tb),
        grid=(t // tb,),
        in_specs=[smem, tok, vrow, vrow, const(summat.shape), table],
        out_specs=vrow,
        out_shape=jax.ShapeDtypeStruct((t, _N_PAIRS), F32),
        scratch_shapes=[pltpu.VMEM((tb, _KROWS), F32)],
        compiler_params=_params(("arbitrary",)),
        name="peer_hidden",
    )(row, xn, par, gate, summat, u_tab)
    y = pl.pallas_call(
        functools.partial(_peer_out_kernel, tb=tb),
        grid=(t // tb,),
        in_specs=[smem, vrow, vrow, tok, const((8, LANES)), const(exlo.shape), const(exhi.shape), table],
        out_specs=pl.BlockSpec((tb, 8, LANES), lambda i: (i, 0, 0)),
        out_shape=jax.ShapeDtypeStruct((t, 8, LANES), F32),
        scratch_shapes=[pltpu.VMEM((tb, _KROWS), F32)],
        compiler_params=_params(("arbitrary",)),
        name="peer_out",
    )(row, par, w, x1, norm_final.reshape(8, LANES), exlo, exhi, v_tab)
    return y.reshape(t, D_MODEL)


_PAGES_PER_STEP = 8


def _head_rows(n_rows, per_head, values_ref, base):
    rowi = lax.broadcasted_iota(I32, (n_rows, 1), 0)
    col = jnp.zeros((n_rows, 1), F32)
    for h in range(n_rows // per_head):
        col = jnp.where(rowi // per_head == h, values_ref[base + h], col)
    return col


def _paged_fetch(pt_ref, sem, make_copies, pages_per_step):
    b, st = pl.program_id(0), pl.program_id(1)
    n_st = pl.num_programs(1)
    step = b * n_st + st
    slot = step % 2

    @pl.when(step == 0)
    def _():
        for cp in make_copies(b, st * pages_per_step, slot):
            cp.start()

    @pl.when(step + 1 < pl.num_programs(0) * n_st)
    def _():
        nxt = step + 1
        for cp in make_copies(nxt // n_st, (nxt % n_st) * pages_per_step, 1 - slot):
            cp.start()

    for cp in make_copies(b, st * pages_per_step, slot):
        cp.wait()
    del sem
    return slot


def _diff_sample_kernel(pt_ref, lam_ref, slopes_ref, cache_ref, wq_ref, new_ref, subln_ref, o_ref,
                        m_ref, l_ref, acc_ref, buf_ref, sem, *, nq, past, page, lam_init):
    def copies(b, p0, slot):
        return [pltpu.make_async_copy(cache_ref.at[0, pt_ref[b, p0 + i], :, h, :],
                                      buf_ref.at[slot, h, pl.ds(i * page, page), :], sem.at[slot])
                for i in range(_PAGES_PER_STEP) for h in range(A_HEADS)]

    slot = _paged_fetch(pt_ref, sem, copies, _PAGES_PER_STEP)
    st = pl.program_id(1)
    hr = 2 * nq
    n_rows = A_HEADS * hr
    slope_col = _head_rows(n_rows, hr, slopes_ref, 0)
    rowi = lax.broadcasted_iota(I32, (n_rows, 1), 0)
    qpos_i = past + rowi % nq
    qpos = qpos_i.astype(F32)
    wq = wq_ref[...]

    @pl.when(st == 0)
    def _():
        m_ref[...] = jnp.full_like(m_ref, NEG)
        l_ref[...] = jnp.zeros_like(l_ref)
        acc_ref[...] = jnp.zeros_like(acc_ref)

    def update(ks, vs, k0, n, mask):
        kpos_i = k0 + lax.broadcasted_iota(I32, (1, n), 1)
        s = jnp.concatenate([_nt(wq[h * hr:(h + 1) * hr], ks[h]) for h in range(A_HEADS)], axis=0)
        s = s - slope_col * (qpos - kpos_i.astype(F32))
        if mask is not None:
            s = jnp.where(mask(kpos_i), s, NEG)
        m = m_ref[...]
        mn = jnp.maximum(m, jnp.max(s, axis=1, keepdims=True))
        c = jnp.exp(m - mn)
        p = jnp.exp(s - mn)
        pb = p.astype(BF16)
        pv = jnp.concatenate([jnp.dot(pb[h * hr:(h + 1) * hr], vs[h], preferred_element_type=F32)
                              for h in range(A_HEADS)], axis=0)
        m_ref[...] = mn
        l_ref[...] = l_ref[...] * c + jnp.sum(p, axis=1, keepdims=True)
        acc_ref[...] = acc_ref[...] * c + pv

    update([buf_ref[slot, h, :, 0:LANES].astype(BF16) for h in range(A_HEADS)],
           [buf_ref[slot, h, :, LANES:2 * LANES].astype(BF16) for h in range(A_HEADS)],
           st * _PAGES_PER_STEP * page, _PAGES_PER_STEP * page, None)

    @pl.when(st == pl.num_programs(1) - 1)
    def _():
        new = jnp.concatenate([new_ref[...], jnp.zeros((LANES - nq, new_ref.shape[1]), BF16)], axis=0)
        update([new[:, 2 * h * LANES:(2 * h + 1) * LANES] for h in range(A_HEADS)],
               [new[:, (2 * h + 1) * LANES:(2 * h + 2) * LANES] for h in range(A_HEADS)],
               past, LANES, lambda kp: kp <= qpos_i)
        o = acc_ref[...] / l_ref[...]
        lam = lam_ref[0]
        for h in range(A_HEADS):
            oh = o[h * hr:h * hr + nq] - lam * o[h * hr + nq:(h + 1) * hr]
            y = oh * lax.rsqrt(jnp.mean(oh * oh, axis=-1, keepdims=True) + NORM_EPS) * subln_ref[...]
            o_ref[:, h * A_VD:(h + 1) * A_VD] = (y * (1.0 - lam_init)).astype(BF16)


def _diff_attn_sample(aq, akvb, cache_a, page_table, lam, subln, db, nq, lam_init):
    n_phys, page = cache_a.shape[1], cache_a.shape[2]
    n_pages = page_table.shape[1]
    past = n_pages * page
    assert n_pages % _PAGES_PER_STEP == 0 and nq == 8
    n_rows = A_HEADS * 2 * nq
    q5 = aq.reshape(db, nq, A_HEADS, 2, A_DH).transpose(0, 2, 3, 1, 4)
    wq = jnp.einsum('bhcqd,cC->bhcqCd', q5, jnp.eye(2, dtype=BF16)).reshape(db, n_rows, 2 * A_DH)
    per_b = lambda r, c: pl.BlockSpec((None, r, c), lambda b, st, pt: (b, 0, 0))
    grid_spec = pltpu.PrefetchScalarGridSpec(
        num_scalar_prefetch=1,
        grid=(db, n_pages // _PAGES_PER_STEP),
        in_specs=[pl.BlockSpec(memory_space=pltpu.SMEM), pl.BlockSpec(memory_space=pltpu.SMEM),
                  pl.BlockSpec(memory_space=pl.ANY),
                  per_b(n_rows, 2 * A_DH), per_b(nq, 1024), pl.BlockSpec((1, A_VD), lambda b, st, pt: (0, 0))],
        out_specs=per_b(nq, 512),
        scratch_shapes=[pltpu.VMEM((n_rows, 1), F32), pltpu.VMEM((n_rows, 1), F32), pltpu.VMEM((n_rows, A_VD), F32),
                        pltpu.VMEM((2, A_HEADS, _PAGES_PER_STEP * page, 2 * LANES), F32),
                        pltpu.SemaphoreType.DMA((2,))],
    )
    out = pl.pallas_call(
        functools.partial(_diff_sample_kernel, nq=nq, past=past, page=page, lam_init=lam_init),
        grid_spec=grid_spec,
        out_shape=jax.ShapeDtypeStruct((db, nq, 512), BF16),
        compiler_params=_params(("arbitrary", "arbitrary")),
        name="diff_attn_sample",
    )(page_table, lam.reshape(1), jnp.asarray(_alibi(A_HEADS)), cache_a, wq,
      akvb.reshape(db, nq, 1024), subln.reshape(1, A_VD))
    return out.reshape(db * nq, 512)


def _compress_sample(cache_cmp, page_table, wbig, w2big, pec, db):
    n_phys, page = cache_cmp.shape[1], cache_cmp.shape[2]
    n_pages = page_table.shape[1]
    pps = math.gcd(n_pages, 32)
    assert page % (8 * CMP_STRIDE) == 0
    m = pps * page // CMP_STRIDE
    n_ch = n_pages * page // CMP_STRIDE
    del n_phys
    const = lambda shp: pl.BlockSpec(shp, lambda b, st, pt: (0, 0))
    grid_spec = pltpu.PrefetchScalarGridSpec(
        num_scalar_prefetch=1,
        grid=(db, n_pages // pps),
        in_specs=[pl.BlockSpec(memory_space=pl.ANY), const(wbig.shape), const(pec.shape), const(w2big.shape)],
        out_specs=pl.BlockSpec((None, m, 256), lambda b, st, pt: (b, st, 0)),
        scratch_shapes=[pltpu.VMEM((8, 512), F32), pltpu.VMEM((2, B_KV * pps, page, LANES), F32),
                        pltpu.SemaphoreType.DMA((2,))],
    )
    return pl.pallas_call(
        functools.partial(_compress_sample_kernel, n_in=pps, rows=page),
        grid_spec=grid_spec,
        out_shape=jax.ShapeDtypeStruct((db, n_ch, 256), BF16),
        compiler_params=_params(("arbitrary", "arbitrary")),
        name="nsa_compress_sample",
    )(page_table, cache_cmp, wbig, pec, w2big)


def _nsa_sample_kernel(pt_ref, slopes_ref, cache_ref, wq_ref, ckv_ref, win_ref, snew_ref, wnew_ref, gates_ref,
                       ovt_ref, onorm_ref, o_ref, neg_ref, oc_ref, m_ref, l_ref, acc_ref, buf_ref, sem,
                       *, nq, past, page, n_slc):
    def copies(b, p0, slot):
        return [pltpu.make_async_copy(cache_ref.at[0, pt_ref[b, p0 + i], :, g, :],
                                      buf_ref.at[slot, pl.ds(i * page, page), pl.ds(g * LANES, LANES)], sem.at[slot])
                for i in range(_PAGES_PER_STEP) for g in range(B_KV)]

    slot = _paged_fetch(pt_ref, sem, copies, _PAGES_PER_STEP)
    st = pl.program_id(1)
    n_rows = B_HEADS * nq
    n_j = neg_ref.shape[1]
    slope_col = _head_rows(n_rows, nq, slopes_ref, 0)
    rowi = lax.broadcasted_iota(I32, (n_rows, 1), 0)
    qpos_i = past + rowi % nq
    qpos = qpos_i.astype(F32)
    wq = wq_ref[...]
    init = (jnp.full((n_rows, 1), NEG, F32), jnp.zeros((n_rows, 1), F32), jnp.zeros((n_rows, 256), F32))

    def pad_rows(x):
        return jnp.concatenate([x, jnp.zeros((LANES - x.shape[0], x.shape[1]), x.dtype)], axis=0)

    def scores(rows_bf, kpos_i):
        return _nt(wq, rows_bf) - slope_col * (qpos - kpos_i.astype(F32))

    def block_bias(blk_of_key):
        n = blk_of_key.shape[1]
        onehot = (lax.broadcasted_iota(I32, (n_j, n), 0) == blk_of_key).astype(BF16)
        return jnp.dot(neg_ref[...], onehot, preferred_element_type=F32)

    @pl.when(st == 0)
    def _():
        ckv = ckv_ref[...]
        n_ch = ckv.shape[0]
        mcol = lax.broadcasted_iota(I32, (1, n_ch), 1)
        cpos = mcol * CMP_STRIDE + (CMP_STRIDE - 1)
        valid = (mcol >= 1) & (cpos <= qpos_i)
        s = jnp.where(valid, scores(ckv, cpos), NEG)
        p = jnp.exp(s - jnp.max(s, axis=1, keepdims=True))
        p = jnp.where(qpos_i >= CMP_LEN - 1, p / jnp.sum(p, axis=1, keepdims=True), 0.0)
        oc_ref[...] = jnp.dot(p.astype(BF16), ckv, preferred_element_type=F32)
        grp = B_REP * nq
        psum = jnp.concatenate(
            [sum(p[g * grp + r * nq:g * grp + (r + 1) * nq] for r in range(B_REP)) for g in range(B_KV)], axis=0)
        hi = psum.astype(BF16)
        lo = (psum - hi.astype(F32)).astype(BF16)
        imp_t = _nt(ovt_ref[...], hi) + _nt(ovt_ref[...], lo)
        shp = imp_t.shape
        jrow = lax.broadcasted_iota(I32, shp, 0)
        jcur = (past + lax.broadcasted_iota(I32, shp, 1) % nq) // SEL_BLOCK
        forced = (jrow == 0) | (jrow == jcur) | (jrow == jcur - 1)
        score = jnp.where(jrow > jcur, -FORCED, jnp.where(forced, FORCED, imp_t))
        rank = _rank_rows(score, n_slc)
        neg_t = jnp.where(rank < N_SEL, 0.0, NEG)
        neg = jnp.concatenate([neg_t, jnp.zeros((n_j, LANES - shp[1]), F32)], axis=1).T
        neg = jnp.concatenate([neg[g * nq:(g + 1) * nq] for g in range(B_KV) for _ in range(B_REP)], axis=0)
        neg_ref[...] = neg.astype(BF16)
        m_ref[...], l_ref[...], acc_ref[...] = init

    rows = buf_ref[slot].astype(BF16)
    n_keys = _PAGES_PER_STEP * page
    kpos_i = st * n_keys + lax.broadcasted_iota(I32, (1, n_keys), 1)
    m_ref[...], l_ref[...], acc_ref[...] = _online_update(
        (m_ref[...], l_ref[...], acc_ref[...]), scores(rows, kpos_i) + block_bias(kpos_i // SEL_BLOCK), rows)

    @pl.when(st == pl.num_programs(1) - 1)
    def _():
        npos_i = past + lax.broadcasted_iota(I32, (1, LANES), 1)
        causal = npos_i <= qpos_i
        snew = pad_rows(snew_ref[...].astype(BF16))
        s = jnp.where(causal, scores(snew, npos_i) + block_bias(npos_i // SEL_BLOCK), NEG)
        _, l_s, a_s = _online_update((m_ref[...], l_ref[...], acc_ref[...]), s, snew)
        win = win_ref[...].astype(BF16)
        wb = win.shape[0]
        wpos_i = past - wb + lax.broadcasted_iota(I32, (1, wb), 1)
        d = qpos_i - wpos_i
        cw = _online_update(init, jnp.where((d >= 0) & (d < WINDOW), scores(win, wpos_i), NEG), win)
        wnew = pad_rows(wnew_ref[...].astype(BF16))
        _, l_w, a_w = _online_update(cw, jnp.where(causal, scores(wnew, npos_i), NEG), wnew)
        gates = jnp.concatenate([gates_ref[...]] * B_HEADS, axis=0)
        glane = lax.broadcasted_iota(I32, (n_rows, LANES), 1)

        def gate_col(branch):
            return jnp.sum(jnp.where(glane == branch * B_HEADS + rowi // nq, gates, 0.0), axis=1, keepdims=True)

        o = gate_col(0) * oc_ref[...] + gate_col(1) * (a_s / l_s) + gate_col(2) * (a_w / l_w)
        lane = lax.broadcasted_iota(I32, (nq, LANES), 1)
        outs = []
        for hd in range(B_HEADS):
            g = hd // B_REP
            blk = jnp.where(lane >= B_DH, o[hd * nq:(hd + 1) * nq, g * LANES:(g + 1) * LANES], 0.0)
            y = blk * lax.rsqrt(jnp.sum(blk * blk, axis=1, keepdims=True) * (1.0 / B_DH) + NORM_EPS) * onorm_ref[...]
            outs.append(y[:, B_DH:])
        o_ref[...] = jnp.concatenate(outs, axis=1).astype(BF16)


def _overlap_rows(n_rows, n_slc, n_j):
    m = np.arange(n_rows)[None, :]
    start = (m - 1) * CMP_STRIDE
    j = np.arange(n_j)[:, None]
    ov = (start < (j + 1) * SEL_BLOCK) & (start + CMP_LEN > j * SEL_BLOCK) & (m >= 1) & (j < n_slc)
    return jnp.asarray(ov.astype(np.float32), BF16)


def _nsa_sample(bq, ckv, skv, wkv, gates, cache_cmp, cache_sel, state_win, page_table, cmp_pe, cmp_w1, cmp_w2,
                out_norm, db, nq):
    n_phys, page = cache_sel.shape[1], cache_sel.shape[2]
    n_pages = page_table.shape[1]
    past = n_pages * page
    wb = state_win.shape[2]
    assert nq == 8 and nq < CMP_STRIDE and past % SEL_BLOCK == 0 and wb == WINDOW and page % SEL_BLOCK == 0
    n_ch = past // CMP_STRIDE
    n_slc = -(-(past + nq) // SEL_BLOCK)
    n_j = -(-n_slc // LANES) * LANES
    n_rows = B_HEADS * nq
    wbig, w2big, pec = _prep_compress(cmp_pe, cmp_w1, cmp_w2, v_first=False)
    ckvc = _compress_sample(cache_cmp, page_table, wbig, w2big, pec, db)
    q = bq[:, :, B_DH:].reshape(B_KV, B_REP, db, nq, B_DH)
    wq = jnp.einsum('grbqd,gG->bgrqGd', q, jnp.eye(B_KV, dtype=BF16))
    wq = jnp.pad(wq, ((0, 0),) * 5 + ((0, B_DH),)).reshape(db, n_rows, 256)
    onorm = jnp.pad(out_norm, (B_DH, 0)).reshape(1, LANES)
    del n_phys
    per_b = lambda r, c: pl.BlockSpec((None, r, c), lambda b, st, pt: (b, 0, 0))
    const = lambda shp: pl.BlockSpec(shp, lambda b, st, pt: (0, 0))
    grid_spec = pltpu.PrefetchScalarGridSpec(
        num_scalar_prefetch=1,
        grid=(db, n_pages // _PAGES_PER_STEP),
        in_specs=[pl.BlockSpec(memory_space=pltpu.SMEM), pl.BlockSpec(memory_space=pl.ANY),
                  per_b(n_rows, 256), per_b(n_ch, 256), per_b(wb, 256), per_b(nq, 256), per_b(nq, 256),
                  per_b(nq, LANES), const((n_j, n_ch)), const((1, LANES))],
        out_specs=per_b(nq, 512),
        scratch_shapes=[pltpu.VMEM((n_rows, n_j), BF16), pltpu.VMEM((n_rows, 256), F32), pltpu.VMEM((n_rows, 1), F32),
                        pltpu.VMEM((n_rows, 1), F32), pltpu.VMEM((n_rows, 256), F32),
                        pltpu.VMEM((2, _PAGES_PER_STEP * page, B_KV * LANES), F32), pltpu.SemaphoreType.DMA((2,))],
    )
    out = pl.pallas_call(
        functools.partial(_nsa_sample_kernel, nq=nq, past=past, page=page, n_slc=n_slc),
        grid_spec=grid_spec,
        out_shape=jax.ShapeDtypeStruct((db, nq, 512), BF16),
        compiler_params=_params(("arbitrary", "arbitrary")),
        name="nsa_sample",
    )(page_table, jnp.asarray(_alibi(B_HEADS)), cache_sel, wq, ckvc,
      state_win.reshape(db, wb, 256), skv.reshape(db, nq, 256), wkv.reshape(db, nq, 256), gates.reshape(db, nq, LANES),
      _overlap_rows(n_ch, n_slc, n_j), onorm)
    return out.reshape(db * nq, 512)


def kernel(x_prompt, x_sample, cache_a, cache_cmp, cache_sel, state_win, page_table, norm_mix, w_in, diff_lambda,
           diff_subln, cmp_pe, cmp_w1, cmp_w2, nsa_out_norm, w_out, norm_ffn, peer_wq, peer_subkeys, peer_u, peer_v,
           norm_final):
    b, s, _ = x_prompt.shape
    db, nq, _ = x_sample.shape
    lam_init = 0.8 - 0.6 * math.exp(-0.3 * 0)
    dl = diff_lambda[0]
    lam = jnp.exp(jnp.sum(dl[0] * dl[1])) - jnp.exp(jnp.sum(dl[2] * dl[3])) + lam_init
    w = _prep_w_in(w_in[0])
    wbig, w2big, pec = _prep_compress(cmp_pe[0], cmp_w1[0], cmp_w2[0])
    u_tab = _pack_table(peer_u[0])
    v_tab = _pack_table(peer_v[0])

    ts = db * nq
    xs = x_sample.reshape(ts, D_MODEL)
    saq, sakv, sakvb, sbq, sckv, sskv, swkv, _, _, sgates = _in_proj(xs, norm_mix[0], w)
    soa = _diff_attn_sample(saq, sakvb, cache_a, page_table, lam, diff_subln[0], db, nq, lam_init)
    sob = _nsa_sample(sbq, sckv, sskv, swkv, sgates, cache_cmp, cache_sel, state_win, page_table, cmp_pe[0],
                      cmp_w1[0], cmp_w2[0], nsa_out_norm[0], db, nq)
    sx1, sxn, seid, sgate = _route(xs, soa, sob, w_out[0], norm_ffn[0], peer_wq[0], peer_subkeys[0])
    y_sample = _peer_experts(sx1, sxn, seid, sgate, u_tab, v_tab, norm_final).reshape(db, nq, D_MODEL)
    new_win = jnp.concatenate([state_win[0][:, nq:], swkv.reshape(db, nq, B_KV, 128)], axis=1)

    xp = x_prompt.reshape(b * s, D_MODEL)
    aq, akv, akvb, bq, ckv, skv, wkv, skvb, wkvb, gates = _in_proj(xp, norm_mix[0], w)
    oa = _diff_attn_prompt(aq, akvb, lam, diff_subln[0], b, s, lam_init)
    ckvc = _compress_prompt(ckv, b, s, wbig, w2big, pec)
    ob = _nsa_prompt(bq, ckvc, skvb, wkvb, gates, nsa_out_norm[0], b, s)
    x1, xn, eid, gate = _route(xp, oa, ob, w_out[0], norm_ffn[0], peer_wq[0], peer_subkeys[0])
    y_prompt = _peer_experts(x1, xn, eid, gate, u_tab, v_tab, norm_final).reshape(b, s, D_MODEL)
    win = min(WINDOW, s)
    outs_p = (akv.reshape(1, b, s, A_HEADS, 256), ckv.reshape(1, b, s, B_KV, 128), skv.reshape(1, b, s, B_KV, 128),
              wkv.reshape(b, s, B_KV, 128)[None, :, s - win:])
    return (y_prompt, y_sample, outs_p[0], sakv.reshape(1, db, nq, A_HEADS, 256), outs_p[1],
            sckv.reshape(1, db, nq, B_KV, 128), outs_p[2], sskv.reshape(1, db, nq, B_KV, 128), outs_p[3],
            new_win[None])
```

```python
import functools
import math

import jax
import jax.numpy as jnp
import numpy as np
from jax import lax
from jax.experimental import pallas as pl
from jax.experimental.pallas import tpu as pltpu

F32 = jnp.float32
BF16 = jnp.bfloat16
I32 = jnp.int32

D_MODEL = 1024
A_HEADS = 4
A_DH = 64
A_VD = 128
B_HEADS = 8
B_DH = 64
B_KV = 2
B_REP = 4
CMP_STRIDE = 16
CMP_LEN = 32
CMP_HID = 128
SEL_BLOCK = 64
N_SEL = 16
WINDOW = 512
PEER_HEADS = 8
PEER_NKEYS = 128
PEER_DK = 256
PEER_TOPK = 16
NORM_EPS = 1e-6
NEG = -1e30
FORCED = 1e9

VMEM_LIMIT_V7X = 52 * 1024 * 1024
LANES = 128

_C_AQ = 0
_C_AKV = 512
_C_BQ = 1536
_C_CKV = 2560
_C_SKV = 2816
_C_WKV = 3072
_C_GATE = 3328
_C_END = 3456


def _nt(a, b):
    return lax.dot_general(a, b, (((1,), (1,)), ((), ())), preferred_element_type=F32)


def _gelu(x):
    return 0.5 * x * (1.0 + lax.erf(x * 0.7071067811865476))


def _params(sem):
    return pltpu.CompilerParams(dimension_semantics=sem, vmem_limit_bytes=VMEM_LIMIT_V7X)


def _alibi(n):
    return np.asarray(2.0 ** (-8.0 * np.arange(1, n + 1) / n), dtype=np.float32)


def _prep_w_in(w_in):
    d = w_in.shape[0]
    aq = w_in[:, 0:512] * 0.125
    ak = w_in[:, 512:1024].reshape(d, A_HEADS, 2 * A_DH)
    av = w_in[:, 1024:1536].reshape(d, A_HEADS, A_VD)
    akv = jnp.concatenate([ak, av], axis=-1).reshape(d, 1024)
    bq = w_in[:, 1536:2048].reshape(d, B_HEADS, B_DH) * 0.125
    bq = jnp.concatenate([jnp.zeros_like(bq), bq], axis=-1).reshape(d, 1024)
    ckv = w_in[:, 2048:2304]
    skv = w_in[:, 2304:2560]
    wkv = w_in[:, 2560:2816]
    gt = w_in[:, 2816:2840].reshape(d, B_HEADS, 3).transpose(0, 2, 1).reshape(d, 24)
    gt = jnp.pad(gt, ((0, 0), (0, LANES - 24)))
    return jnp.concatenate([aq, akv, bq, ckv, skv, wkv, gt], axis=1).astype(BF16)


def _inproj_kernel(x_ref, g_ref, w_ref, aq_ref, akv_ref, akvb_ref, bq_ref, ckv_ref, skv_ref, wkv_ref,
                   skvb_ref, wkvb_ref, gate_ref):
    x = x_ref[...]
    xn = x * lax.rsqrt(jnp.mean(x * x, axis=-1, keepdims=True) + NORM_EPS) * g_ref[...]
    xb = xn.astype(BF16)

    def seg(a, b):
        return jnp.dot(xb, w_ref[:, a:b], preferred_element_type=F32)

    aq_ref[...] = seg(_C_AQ, _C_AKV).astype(BF16)
    akv = seg(_C_AKV, _C_BQ)
    akv_ref[...] = akv
    akvb_ref[...] = akv.astype(BF16)
    bq = seg(_C_BQ, _C_CKV).astype(BF16)
    for hd in range(B_HEADS):
        bq_ref[hd] = bq[:, hd * LANES:(hd + 1) * LANES]
    ckv_ref[...] = seg(_C_CKV, _C_SKV)
    for (a, fref, bref) in ((_C_SKV, skv_ref, skvb_ref), (_C_WKV, wkv_ref, wkvb_ref)):
        kv = seg(a, a + 2 * LANES)
        fref[...] = kv
        sw = jnp.concatenate([pltpu.roll(kv[:, g * LANES:(g + 1) * LANES], 64, 1) for g in range(B_KV)], axis=1)
        bref[...] = sw.astype(BF16)
    gate_ref[...] = jax.nn.sigmoid(seg(_C_GATE, _C_END))


def _in_proj(x, gain, w):
    t = x.shape[0]
    tm = 256
    assert t % tm == 0
    row = lambda n: pl.BlockSpec((tm, n), lambda i: (i, 0))
    out_shape = (
        jax.ShapeDtypeStruct((t, 512), BF16),
        jax.ShapeDtypeStruct((t, 1024), F32),
        jax.ShapeDtypeStruct((t, 1024), BF16),
        jax.ShapeDtypeStruct((B_HEADS, t, LANES), BF16),
        jax.ShapeDtypeStruct((t, 256), F32),
        jax.ShapeDtypeStruct((t, 256), F32),
        jax.ShapeDtypeStruct((t, 256), F32),
        jax.ShapeDtypeStruct((t, 256), BF16),
        jax.ShapeDtypeStruct((t, 256), BF16),
        jax.ShapeDtypeStruct((t, LANES), F32),
    )
    out_specs = (row(512), row(1024), row(1024), pl.BlockSpec((B_HEADS, tm, LANES), lambda i: (0, i, 0)),
                 row(256), row(256), row(256), row(256), row(256), row(LANES))
    return pl.pallas_call(
        _inproj_kernel,
        grid=(t // tm,),
        in_specs=[row(D_MODEL), pl.BlockSpec((1, D_MODEL), lambda i: (0, 0)),
                  pl.BlockSpec((D_MODEL, _C_END), lambda i: (0, 0))],
        out_specs=out_specs,
        out_shape=out_shape,
        compiler_params=_params(("parallel",)),
        name="in_proj",
    )(x, gain.reshape(1, D_MODEL), w)


def _online_update(carry, s, v):
    m, l, acc = carry
    mn = jnp.maximum(m, jnp.max(s, axis=1, keepdims=True))
    c = jnp.exp(m - mn)
    p = jnp.exp(s - mn)
    l = l * c + jnp.sum(p, axis=1, keepdims=True)
    acc = acc * c + jnp.dot(p.astype(BF16), v, preferred_element_type=F32)
    return mn, l, acc


def _diff_prompt_kernel(lam_ref, slopes_ref, q_ref, kv_ref, subln_ref, o_ref, *, tq, tk, lam_init):
    qi = pl.program_id(1)
    q0 = qi * tq
    lam = lam_ref[0]
    lane = lax.broadcasted_iota(I32, (tq, LANES), 1)
    rel = (lax.broadcasted_iota(I32, (tq, tk), 0) - lax.broadcasted_iota(I32, (tq, tk), 1))
    relf = rel.astype(F32)
    qs = []
    for h in range(A_HEADS):
        q = q_ref[:, h * LANES:(h + 1) * LANES]
        zero = jnp.zeros_like(q)
        qs.append((jnp.where(lane < A_DH, q, zero), jnp.where(lane >= A_DH, q, zero)))

    def tile(kt, carry, diag):
        k0 = pl.multiple_of(kt * tk, tk)
        dist = relf + (q0 - k0).astype(F32)
        out = []
        for h in range(A_HEADS):
            kk = kv_ref[pl.ds(k0, tk), 2 * h * LANES:(2 * h + 1) * LANES]
            v = kv_ref[pl.ds(k0, tk), (2 * h + 1) * LANES:(2 * h + 2) * LANES]
            bias = slopes_ref[h] * dist
            for c in range(2):
                s = _nt(qs[h][c], kk) - bias
                if diag:
                    s = jnp.where(dist >= 0, s, NEG)
                out.append(_online_update(carry[2 * h + c], s, v))
        return tuple(out)

    init1 = (jnp.full((tq, 1), NEG, F32), jnp.zeros((tq, 1), F32), jnp.zeros((tq, A_VD), F32))
    n_full = q0 // tk
    carry = lax.fori_loop(0, n_full, lambda kt, c: tile(kt, c, False), (init1,) * (2 * A_HEADS))
    for d in range(max(tq // tk, 1)):
        carry = tile(n_full + d, carry, True)
    for h in range(A_HEADS):
        (_, l1, a1), (_, l2, a2) = carry[2 * h], carry[2 * h + 1]
        o = a1 / l1 - lam * (a2 / l2)
        y = o * lax.rsqrt(jnp.mean(o * o, axis=-1, keepdims=True) + NORM_EPS) * subln_ref[...]
        o_ref[:, h * LANES:(h + 1) * LANES] = (y * (1.0 - lam_init)).astype(BF16)


def _diff_attn_prompt(aq, akvb, lam, subln, b, s, lam_init):
    tq, tk = 512, 512
    assert s % tq == 0 and s % tk == 0 and (tq % tk == 0 or tk % tq == 0)
    aq = aq.reshape(b, s, 512)
    akvb = akvb.reshape(b, s, 1024)
    out = pl.pallas_call(
        functools.partial(_diff_prompt_kernel, tq=tq, tk=tk, lam_init=lam_init),
        grid=(b, s // tq),
        in_specs=[pl.BlockSpec(memory_space=pltpu.SMEM), pl.BlockSpec(memory_space=pltpu.SMEM),
                  pl.BlockSpec((None, tq, 512), lambda bi, qi: (bi, qi, 0)),
                  pl.BlockSpec((None, s, 1024), lambda bi, qi: (bi, 0, 0)),
                  pl.BlockSpec((1, A_VD), lambda bi, qi: (0, 0))],
        out_specs=pl.BlockSpec((None, tq, 512), lambda bi, qi: (bi, qi, 0)),
        out_shape=jax.ShapeDtypeStruct((b, s, 512), BF16),
        compiler_params=_params(("parallel", "arbitrary")),
        name="diff_attn_prompt",
    )(lam.reshape(1), jnp.asarray(_alibi(A_HEADS)), aq, akvb, subln.reshape(1, A_VD))
    return out.reshape(b * s, 512)


def _prep_compress(cmp_pe, cmp_w1, cmp_w2, v_first=True):
    w1r = cmp_w1.reshape(2, 2, CMP_STRIDE, B_DH, CMP_HID)
    base = w1r.transpose(2, 0, 3, 1, 4)
    wbig = jnp.zeros((CMP_STRIDE, B_KV, 2, B_DH, 2, B_KV, 2, CMP_HID), F32)
    w2big = jnp.zeros((B_KV, 2, CMP_HID, B_KV, 2, B_DH), F32)
    for g in range(B_KV):
        for e in range(2):
            wbig = wbig.at[:, g, e, :, :, g, e, :].set(base[:, e])
            w2big = w2big.at[g, e, :, g, (1 - e) if v_first else e, :].set(cmp_w2[e])
    wbig = wbig.reshape(CMP_STRIDE * 256, 1024).astype(BF16)
    w2big = w2big.reshape(512, 256).astype(BF16)
    per = cmp_pe.reshape(2, 2, CMP_STRIDE, B_DH)
    pec = jnp.broadcast_to(per.transpose(1, 2, 0, 3)[:, :, None], (2, CMP_STRIDE, B_KV, 2, B_DH))
    pec = jnp.pad(pec.reshape(2, CMP_STRIDE * 256), ((0, 6), (0, 0))).astype(BF16)
    return wbig, w2big, pec


def _compress_kernel(*refs, n_in, rows):
    x_refs = refs[-(2 * n_in + 5):-5]
    m_each = rows // CMP_STRIDE
    _compress_core(lambda i, g, s: x_refs[2 * i + g][pl.ds(s, m_each, stride=CMP_STRIDE), :], n_in, *refs[-5:])


def _compress_sample_kernel(pt_ref, cache_ref, w1_ref, pe_ref, w2_ref, o_ref, prev_ref, buf_ref, sem, *, n_in, rows):
    def copies(b, p0, slot):
        return [pltpu.make_async_copy(cache_ref.at[0, pt_ref[b, p0 + i], :, g, :], buf_ref.at[slot, 2 * i + g],
                                      sem.at[slot])
                for i in range(n_in) for g in range(B_KV)]

    slot = _paged_fetch(pt_ref, sem, copies, n_in)
    m_each = rows // CMP_STRIDE
    _compress_core(lambda i, g, s: buf_ref[slot, 2 * i + g, pl.ds(s, m_each, stride=CMP_STRIDE), :], n_in,
                   w1_ref, pe_ref, w2_ref, o_ref, prev_ref)


def _compress_core(chunk_rows, n_in, w1_ref, pe_ref, w2_ref, o_ref, prev_ref):
    step = pl.program_id(1)

    @pl.when(step == 0)
    def _():
        prev_ref[...] = jnp.zeros_like(prev_ref)

    pieces = []
    for i in range(n_in):
        cols = [chunk_rows(i, g, s).astype(BF16) for s in range(CMP_STRIDE) for g in range(B_KV)]
        pieces.append(jnp.concatenate(cols, axis=1))
    c = pieces[0] if n_in == 1 else jnp.concatenate(pieces, axis=0)
    m = c.shape[0]
    p = jnp.dot(c, w1_ref[...], preferred_element_type=F32)
    pp = jnp.dot(pe_ref[...], w1_ref[...], preferred_element_type=F32)
    pe_term = pp[0:1, :512] + pp[1:2, 512:]
    p0 = p[:, :512]
    p1 = p[:, 512:]
    row = lax.broadcasted_iota(I32, (m, 512), 0)
    p0s = jnp.where(row == 0, prev_ref[7:8, :], pltpu.roll(p0, 1, 0))
    prev_ref[...] = p0[m - 8:m, :]
    act = _gelu(p0s + p1 + pe_term)
    o_ref[...] = jnp.dot(act.astype(BF16), w2_ref[...], preferred_element_type=F32).astype(BF16)


def _compress_prompt(ckv, b, s, wbig, w2big, pec):
    rows = min(s, 2048)
    assert s % rows == 0 and rows % (8 * CMP_STRIDE) == 0
    n_ch = s // CMP_STRIDE
    m = rows // CMP_STRIDE
    const = lambda shp: pl.BlockSpec(shp, lambda bi, st: (0, 0))
    return pl.pallas_call(
        functools.partial(_compress_kernel, n_in=1, rows=rows),
        grid=(b, s // rows),
        in_specs=[pl.BlockSpec((None, rows, LANES), lambda bi, st: (bi, st, 0)),
                  pl.BlockSpec((None, rows, LANES), lambda bi, st: (bi, st, 1)),
                  const(wbig.shape), const(pec.shape), const(w2big.shape)],
        out_specs=pl.BlockSpec((None, m, 256), lambda bi, st: (bi, st, 0)),
        out_shape=jax.ShapeDtypeStruct((b, n_ch, 256), BF16),
        scratch_shapes=[pltpu.VMEM((8, 512), F32)],
        compiler_params=_params(("parallel", "arbitrary")),
        name="nsa_compress_prompt",
    )(ckv.reshape(b, s, 256), ckv.reshape(b, s, 256), wbig, pec, w2big)


def _rank_rows(score, n_rows):
    jrow = lax.broadcasted_iota(I32, score.shape, 0)
    rank = jnp.zeros(score.shape, I32)
    for j2 in range(n_rows):
        sj = score[j2:j2 + 1, :]
        rank = rank + jnp.where(jrow > j2, (sj >= score).astype(I32), (sj > score).astype(I32))
    return rank


def _nsa_prompt_kernel(slopes_ref, q_ref, ckv_ref, ske_ref, skv_ref, wkv_ref, gates_ref, ovt_ref, onorm_ref,
                       o_ref, *, tq, tk, n_slc):
    qi = pl.program_id(1)
    q0 = qi * tq
    mrows = B_REP * tq
    n_ch = ckv_ref.shape[0]
    rowi = lax.broadcasted_iota(I32, (mrows, 1), 0)
    qpos_i = q0 + rowi % tq
    qpos = qpos_i.astype(F32)
    lane = lax.broadcasted_iota(I32, (mrows, LANES), 1)
    n_jr = -(-n_slc // 8) * 8
    jrow = lax.broadcasted_iota(I32, (n_jr, tq), 0)
    jcur = (q0 + lax.broadcasted_iota(I32, (n_jr, tq), 1)) // SEL_BLOCK
    forced = (jrow == 0) | (jrow == jcur) | (jrow == jcur - 1)
    mcol = lax.broadcasted_iota(I32, (1, n_ch), 1)
    cpos = jnp.where(mcol >= 1, mcol * CMP_STRIDE + (CMP_STRIDE - 1), 2 ** 30).astype(F32)

    def lanes_of(g):
        return slice(g * LANES, (g + 1) * LANES)

    def prepare(g):
        q = q_ref[g * B_REP:(g + 1) * B_REP].reshape(mrows, LANES)
        slope_col = jnp.zeros((mrows, 1), F32)
        for r in range(B_REP):
            slope_col = jnp.where(rowi // tq == r, slopes_ref[g * B_REP + r], slope_col)
        ckv = ckv_ref[:, lanes_of(g)]
        s = _nt(q, ckv)
        s = jnp.where(cpos <= qpos, s - slope_col * (qpos - cpos), NEG)
        p = jnp.exp(s - jnp.max(s, axis=1, keepdims=True))
        p = jnp.where(qpos_i >= CMP_LEN - 1, p / jnp.sum(p, axis=1, keepdims=True), 0.0)
        o_c = jnp.dot(p.astype(BF16), ckv, preferred_element_type=F32)
        psum = p[0:tq] + p[tq:2 * tq] + p[2 * tq:3 * tq] + p[3 * tq:4 * tq]
        hi = psum.astype(BF16)
        lo = (psum - hi.astype(F32)).astype(BF16)
        imp_t = _nt(ovt_ref[...], hi) + _nt(ovt_ref[...], lo)
        score = jnp.where(jrow > jcur, -FORCED, jnp.where(forced, FORCED, imp_t[0:n_jr]))
        neg = jnp.where(_rank_rows(score, n_slc) < N_SEL, 0.0, NEG)
        neg = jnp.concatenate([neg, jnp.zeros((LANES - n_jr, tq), F32)], axis=0).T
        neg4 = jnp.concatenate([neg] * B_REP, axis=0).astype(BF16)
        q_aug = jnp.where(lane < B_DH, neg4, q)
        return q, q_aug, slope_col, o_c

    grp = [prepare(g) for g in range(B_KV)]

    def attend(g, carry, qq, kk, vv, k0, n, mask_fn):
        kpos_i = k0 + lax.broadcasted_iota(I32, (1, n), 1)
        sc = _nt(qq, kk) - grp[g][2] * (qpos - kpos_i.astype(F32))
        if mask_fn is not None:
            sc = jnp.where(mask_fn(kpos_i), sc, NEG)
        return _online_update(carry, sc, vv)

    init = (jnp.full((mrows, 1), NEG, F32), jnp.zeros((mrows, 1), F32), jnp.zeros((mrows, LANES), F32))

    def sel_tile(kt, carries, diag):
        k0 = pl.multiple_of(kt * tk, tk)
        return tuple(attend(g, carries[g], grp[g][1], ske_ref[pl.ds(k0, tk), lanes_of(g)],
                            skv_ref[pl.ds(k0, tk), lanes_of(g)], k0, tk, (lambda kp: kp <= qpos_i) if diag else None)
                     for g in range(B_KV))

    n_full = q0 // tk
    carries = lax.fori_loop(0, n_full, lambda kt, c: sel_tile(kt, c, False), (init,) * B_KV)
    carries = sel_tile(n_full, carries, True)

    wlen = min(WINDOW + tq, wkv_ref.shape[0])
    k0w = pl.multiple_of(jnp.maximum(q0 + tq - wlen, 0), tq)

    def wmask(kp):
        return (qpos_i - kp).astype(jnp.uint32) < WINDOW

    gates = gates_ref[...]
    glane = lax.broadcasted_iota(I32, (tq, LANES), 1)
    for g in range(B_KV):
        q, _, _, o_c = grp[g]
        _, l_s, a_s = carries[g]
        kvw = wkv_ref[pl.ds(k0w, wlen), lanes_of(g)]
        _, l_w, a_w = attend(g, init, q, kvw, kvw, k0w, wlen, wmask)

        def gate_col(branch, g=g):
            cols = [jnp.sum(jnp.where(glane == branch * B_HEADS + g * B_REP + r, gates, 0.0), axis=1, keepdims=True)
                    for r in range(B_REP)]
            return jnp.concatenate(cols, axis=0)

        o = gate_col(0) * o_c + gate_col(1) * (a_s / l_s) + gate_col(2) * (a_w / l_w)
        o = jnp.where(lane < B_DH, o, 0.0)
        y = o * lax.rsqrt(jnp.sum(o * o, axis=1, keepdims=True) * (1.0 / B_DH) + NORM_EPS) * onorm_ref[...]
        o_ref[g * B_REP:(g + 1) * B_REP] = y.reshape(B_REP, tq, LANES)[:, :, :B_DH].astype(BF16)


def _overlap_t(n_rows, n_slc):
    m = np.arange(n_rows)[None, :]
    start = (m - 1) * CMP_STRIDE
    j = np.arange(LANES)[:, None]
    ov = (start < (j + 1) * SEL_BLOCK) & (start + CMP_LEN > j * SEL_BLOCK) & (m >= 1) & (j < n_slc)
    return jnp.asarray(ov.astype(np.float32), BF16)


def _nsa_prompt(bq, ckvc, skvb, wkvb, gates, out_norm, b, s):
    tq, tk = 128, 512
    tk = min(tk, s)
    assert s % tk == 0 and tk % tq == 0 and s >= WINDOW
    n_ch = s // CMP_STRIDE
    n_slc = s // SEL_BLOCK
    assert n_slc <= B_DH
    blk = np.arange(s) // SEL_BLOCK
    onehot = jnp.asarray((blk[:, None] == np.arange(LANES)[None, :]).astype(np.float32), BF16)
    skvb = skvb.reshape(b, s, 256)
    lane = jnp.arange(256) % LANES
    ske = jnp.where(lane < B_DH, jnp.concatenate([onehot, onehot], axis=1)[None], skvb)
    onorm = jnp.pad(out_norm, (0, LANES - B_DH)).reshape(1, LANES)
    per_b = lambda n: pl.BlockSpec((None, n, B_KV * LANES), lambda bi, qi: (bi, 0, 0))
    out = pl.pallas_call(
        functools.partial(_nsa_prompt_kernel, tq=tq, tk=tk, n_slc=n_slc),
        grid=(b, s // tq),
        in_specs=[pl.BlockSpec(memory_space=pltpu.SMEM),
                  pl.BlockSpec((B_HEADS, None, tq, LANES), lambda bi, qi: (0, bi, qi, 0)),
                  per_b(n_ch), per_b(s), per_b(s), per_b(s),
                  pl.BlockSpec((None, tq, LANES), lambda bi, qi: (bi, qi, 0)),
                  pl.BlockSpec((LANES, n_ch), lambda bi, qi: (0, 0)),
                  pl.BlockSpec((1, LANES), lambda bi, qi: (0, 0))],
        out_specs=pl.BlockSpec((B_HEADS, None, tq, B_DH), lambda bi, qi: (0, bi, qi, 0)),
        out_shape=jax.ShapeDtypeStruct((B_HEADS, b, s, B_DH), BF16),
        compiler_params=_params(("parallel", "arbitrary")),
        name="nsa_prompt",
    )(jnp.asarray(_alibi(B_HEADS)), bq.reshape(B_HEADS, b, s, LANES), ckvc, ske, skvb,
      wkvb.reshape(b, s, 256), gates.reshape(b, s, LANES), _overlap_t(n_ch, n_slc), onorm)
    return out.transpose(1, 2, 0, 3).reshape(b * s, 512)


def _topk_rows(x, k, payload=None):
    n = x.shape[0]
    rows = lax.broadcasted_iota(I32, x.shape, 0)
    vals, outs = [], []
    cur = x
    for _ in range(k):
        m = jnp.max(cur, axis=0, keepdims=True)
        idx = jnp.min(jnp.where(cur == m, rows, n), axis=0, keepdims=True)
        hit = rows == idx
        vals.append(m)
        outs.append(idx if payload is None else jnp.sum(jnp.where(hit, payload, 0), axis=0, keepdims=True))
        cur = jnp.where(hit, -jnp.inf, cur)
    return jnp.concatenate(vals, axis=0), jnp.concatenate(outs, axis=0)


def _product_key_candidates(hv, hi):
    half = PEER_TOPK // 2
    vals, eids = [], []
    for a in range(half):
        nb = PEER_TOPK // (a + 1)
        rows_b = PEER_TOPK if a == 0 else half
        v = hv[0][a:a + 1] + hv[1][0:rows_b]
        if nb < rows_b:
            v = jnp.where(lax.broadcasted_iota(I32, v.shape, 0) < nb, v, -jnp.inf)
        vals.append(v)
        eids.append(hi[0][a:a + 1] * PEER_NKEYS + hi[1][0:rows_b])
    vals.append(hv[0][half:PEER_TOPK] + hv[1][0:1])
    eids.append(hi[0][half:PEER_TOPK] * PEER_NKEYS + hi[1][0:1])
    return jnp.concatenate(vals, axis=0), jnp.concatenate(eids, axis=0)


def _route_kernel(x_ref, oa_ref, ob_ref, wo_ref, g_ref, wq_ref, keys_ref, x1_ref, xn_ref, eid_ref, gate_ref):
    x1 = (x_ref[...] + jnp.dot(oa_ref[...], wo_ref[0:512, :], preferred_element_type=F32)
          + jnp.dot(ob_ref[...], wo_ref[512:1024, :], preferred_element_type=F32))
    x1_ref[...] = x1
    xn = x1 * lax.rsqrt(jnp.mean(x1 * x1, axis=-1, keepdims=True) + NORM_EPS) * g_ref[...]
    xn_ref[...] = xn
    q = jnp.dot(xn.astype(BF16), wq_ref[...], preferred_element_type=F32).astype(BF16)
    for h in range(PEER_HEADS):
        hv, hi = [], []
        for p in range(2):
            c0 = (h * 2 + p) * LANES
            sc = _nt(keys_ref[p], q[:, c0:c0 + LANES])
            v, i = _topk_rows(sc, PEER_TOPK)
            hv.append(v)
            hi.append(i)
        comb, cand_eid = _product_key_candidates(hv, hi)
        top, eid = _topk_rows(comb, PEER_TOPK, payload=cand_eid)
        e = jnp.exp(top - jnp.max(top, axis=0, keepdims=True))
        eid_ref[h * PEER_TOPK:(h + 1) * PEER_TOPK, :] = eid
        gate_ref[h * PEER_TOPK:(h + 1) * PEER_TOPK, :] = e / jnp.sum(e, axis=0, keepdims=True)


def _route(x, oa, ob, w_out, norm_ffn, peer_wq, peer_subkeys):
    t = x.shape[0]
    tm = 256
    assert t % tm == 0
    nt = t // tm
    row = lambda n: pl.BlockSpec((tm, n), lambda i: (i, 0))
    const = lambda shp: pl.BlockSpec(shp, lambda i: (0,) * len(shp))
    tr = pl.BlockSpec((None, PEER_HEADS * PEER_TOPK, tm), lambda i: (i, 0, 0))
    x1, xn, eid_t, gate_t = pl.pallas_call(
        _route_kernel,
        grid=(nt,),
        in_specs=[row(D_MODEL), row(512), row(512), const((D_MODEL, D_MODEL)), const((1, D_MODEL)),
                  const((D_MODEL, PEER_HEADS * PEER_DK)), const((2, PEER_NKEYS, PEER_DK // 2))],
        out_specs=(row(D_MODEL), row(D_MODEL), tr, tr),
        out_shape=(jax.ShapeDtypeStruct((t, D_MODEL), F32), jax.ShapeDtypeStruct((t, D_MODEL), F32),
                   jax.ShapeDtypeStruct((nt, 128, tm), I32), jax.ShapeDtypeStruct((nt, 128, tm), F32)),
        compiler_params=_params(("parallel",)),
        name="out_proj_route",
    )(x, oa, ob, w_out.astype(BF16), norm_ffn.reshape(1, D_MODEL), peer_wq.astype(BF16), peer_subkeys.astype(BF16))
    eid = eid_t.transpose(0, 2, 1).reshape(t, 128)
    gate = gate_t.transpose(0, 2, 1).reshape(t, 128)
    return x1, xn, eid, gate


_N_PAIRS = PEER_HEADS * PEER_TOPK
_TILE_ROWS = 16


_KROWS = _N_PAIRS * _TILE_ROWS


def _pack_table(tab):
    e = tab.shape[0]
    t = tab.astype(BF16).reshape(e // 2, 2, 8, LANES).transpose(0, 2, 3, 1)
    return lax.bitcast_convert_type(t, jnp.uint32).reshape(e * 4, LANES)


def _tile(tab_ref, r8):
    return pltpu.bitcast(tab_ref[pl.ds(pl.multiple_of(r8, 8), 8), :], BF16)


def _gather_pair(tab_ref, row_ref, ta, tb_):
    ga = jnp.concatenate([_tile(tab_ref, row_ref[ta, j]) for j in range(_N_PAIRS)], axis=0)
    gb = jnp.concatenate([_tile(tab_ref, row_ref[tb_, j]) for j in range(_N_PAIRS)], axis=0)
    return jnp.concatenate([ga, gb], axis=1)


def _row_tile(ref, t):
    sub = lax.broadcasted_iota(I32, (8, LANES), 0)
    rowb = jnp.broadcast_to(ref[pl.ds(t, 1), :], (8, ref.shape[1]))
    tile = rowb[:, 0:LANES]
    for c in range(1, 8):
        tile = jnp.where(sub == c, rowb[:, c * LANES:(c + 1) * LANES], tile)
    return tile


def _diag_mask():
    return ((lax.broadcasted_iota(I32, (8, _KROWS), 1) % _TILE_ROWS) // 2) == lax.broadcasted_iota(I32, (8, _KROWS), 0)


def _peer_hidden_kernel(row_ref, x_ref, par_ref, gate_ref, sum_ref, u_ref, o_ref, d_ref, *, tb):
    diag = _diag_mask()
    zero = jnp.zeros((8, LANES), BF16)

    def pair(i, carry):
        ta = 2 * i
        xa = _row_tile(x_ref, ta).astype(BF16)
        xb = _row_tile(x_ref, ta + 1).astype(BF16)
        lhs = jnp.concatenate([jnp.concatenate([xa, zero], axis=1), jnp.concatenate([zero, xb], axis=1)], axis=0)
        res = _nt(lhs, _gather_pair(u_ref, row_ref, ta, ta + 1))
        d_ref[pl.ds(ta, 1), :] = jnp.sum(jnp.where(diag, res[0:8], 0.0), axis=0, keepdims=True)
        d_ref[pl.ds(ta + 1, 1), :] = jnp.sum(jnp.where(diag, res[8:16], 0.0), axis=0, keepdims=True)
        return carry

    lax.fori_loop(0, tb // 2, pair, 0, unroll=32)
    d = d_ref[...]
    hi = d.astype(BF16)
    lo = (d - hi.astype(F32)).astype(BF16)
    hl = (jnp.dot(hi, sum_ref[...], preferred_element_type=F32)
          + jnp.dot(lo, sum_ref[...], preferred_element_type=F32))
    hdn = jnp.where(par_ref[...] == 1, hl[:, _N_PAIRS:], hl[:, :_N_PAIRS])
    o_ref[...] = gate_ref[...] * _gelu(hdn)


def _peer_out_kernel(row_ref, par_ref, w_ref, x1_ref, gain_ref, exlo_ref, exhi_ref, v_ref, o_ref, wl_ref, *, tb):
    par = par_ref[...].astype(F32)
    w = w_ref[...]
    wl_ref[...] = (jnp.dot((w * (1.0 - par)).astype(BF16), exlo_ref[...], preferred_element_type=F32)
                   + jnp.dot((w * par).astype(BF16), exhi_ref[...], preferred_element_type=F32))
    diag = _diag_mask()

    def finish(t, acc):
        x2 = _row_tile(x1_ref, t) + acc
        ss = jnp.sum(jnp.sum(x2 * x2, axis=1, keepdims=True), axis=0, keepdims=True)
        o_ref[t] = x2 * lax.rsqrt(ss * (1.0 / D_MODEL) + NORM_EPS) * gain_ref[...]

    def pair(i, carry):
        ta = 2 * i
        lhs = jnp.concatenate(
            [jnp.where(diag, jnp.broadcast_to(wl_ref[pl.ds(t, 1), :], (8, _KROWS)), 0.0) for t in (ta, ta + 1)],
            axis=0).astype(BF16)
        res = jnp.dot(lhs, _gather_pair(v_ref, row_ref, ta, ta + 1), preferred_element_type=F32)
        finish(ta, res[0:8, 0:LANES])
        finish(ta + 1, res[8:16, LANES:2 * LANES])
        return carry

    lax.fori_loop(0, tb // 2, pair, 0, unroll=32)


def _peer_consts():
    col = np.arange(_KROWS)
    own = (col[None, :] // _TILE_ROWS) == np.arange(_N_PAIRS)[:, None]
    half = col[None, :] % 2
    exlo = (own & (half == 0)).astype(np.float32)
    exhi = (own & (half == 1)).astype(np.float32)
    summat = np.concatenate([exlo.T, exhi.T], axis=1)
    return jnp.asarray(exlo, BF16), jnp.asarray(exhi, BF16), jnp.asarray(summat, BF16)


def _peer_experts(x1, xn, eid, gate, u_tab, v_tab, norm_final):
    t = x1.shape[0]
    tb = 128
    assert t % tb == 0
    smem = pl.BlockSpec((tb, _N_PAIRS), lambda i: (i, 0), memory_space=pltpu.SMEM)
    vrow = pl.BlockSpec((tb, _N_PAIRS), lambda i: (i, 0))
    tok = pl.BlockSpec((tb, D_MODEL), lambda i: (i, 0))
    table = pl.BlockSpec(memory_space=pltpu.VMEM)
    const = lambda shp: pl.BlockSpec(shp, lambda i: (0,) * len(shp))
    row = (eid >> 1) * 8
    par = eid & 1
    exlo, exhi, summat = _peer_consts()
    w = pl.pallas_call(
        functools.partial(_peer_hidden_kernel, tb=tb),
        grid=(t // tb,),
        in_specs=[smem, tok, vrow, vrow, const(summat.shape), table],
        out_specs=vrow,
        out_shape=jax.ShapeDtypeStruct((t, _N_PAIRS), F32),
        scratch_shapes=[pltpu.VMEM((tb, _KROWS), F32)],
        compiler_params=_params(("arbitrary",)),
        name="peer_hidden",
    )(row, xn, par, gate, summat, u_tab)
    y = pl.pallas_call(
        functools.partial(_peer_out_kernel, tb=tb),
        grid=(t // tb,),
        in_specs=[smem, vrow, vrow, tok, const((8, LANES)), const(exlo.shape), const(exhi.shape), table],
        out_specs=pl.BlockSpec((tb, 8, LANES), lambda i: (i, 0, 0)),
        out_shape=jax.ShapeDtypeStruct((t, 8, LANES), F32),
        scratch_shapes=[pltpu.VMEM((tb, _KROWS), F32)],
        compiler_params=_params(("arbitrary",)),
        name="peer_out",
    )(row, par, w, x1, norm_final.reshape(8, LANES), exlo, exhi, v_tab)
    return y.reshape(t, D_MODEL)


_PAGES_PER_STEP = 8


def _head_rows(n_rows, per_head, values_ref, base):
    rowi = lax.broadcasted_iota(I32, (n_rows, 1), 0)
    col = jnp.zeros((n_rows, 1), F32)
    for h in range(n_rows // per_head):
        col = jnp.where(rowi // per_head == h, values_ref[base + h], col)
    return col


def _paged_fetch(pt_ref, sem, make_copies, pages_per_step):
    b, st = pl.program_id(0), pl.program_id(1)
    n_st = pl.num_programs(1)
    step = b * n_st + st
    slot = step % 2

    @pl.when(step == 0)
    def _():
        for cp in make_copies(b, st * pages_per_step, slot):
            cp.start()

    @pl.when(step + 1 < pl.num_programs(0) * n_st)
    def _():
        nxt = step + 1
        for cp in make_copies(nxt // n_st, (nxt % n_st) * pages_per_step, 1 - slot):
            cp.start()

    for cp in make_copies(b, st * pages_per_step, slot):
        cp.wait()
    del sem
    return slot


def _diff_sample_kernel(pt_ref, lam_ref, slopes_ref, cache_ref, wq_ref, new_ref, subln_ref, o_ref,
                        m_ref, l_ref, acc_ref, buf_ref, sem, *, nq, past, page, lam_init):
    def copies(b, p0, slot):
        return [pltpu.make_async_copy(cache_ref.at[0, pt_ref[b, p0 + i], :, h, :],
                                      buf_ref.at[slot, h, pl.ds(i * page, page), :], sem.at[slot])
                for i in range(_PAGES_PER_STEP) for h in range(A_HEADS)]

    slot = _paged_fetch(pt_ref, sem, copies, _PAGES_PER_STEP)
    st = pl.program_id(1)
    hr = 2 * nq
    n_rows = A_HEADS * hr
    slope_col = _head_rows(n_rows, hr, slopes_ref, 0)
    rowi = lax.broadcasted_iota(I32, (n_rows, 1), 0)
    qpos_i = past + rowi % nq
    qpos = qpos_i.astype(F32)
    wq = wq_ref[...]

    @pl.when(st == 0)
    def _():
        m_ref[...] = jnp.full_like(m_ref, NEG)
        l_ref[...] = jnp.zeros_like(l_ref)
        acc_ref[...] = jnp.zeros_like(acc_ref)

    def update(ks, vs, k0, n, mask):
        kpos_i = k0 + lax.broadcasted_iota(I32, (1, n), 1)
        s = jnp.concatenate([_nt(wq[h * hr:(h + 1) * hr], ks[h]) for h in range(A_HEADS)], axis=0)
        s = s - slope_col * (qpos - kpos_i.astype(F32))
        if mask is not None:
            s = jnp.where(mask(kpos_i), s, NEG)
        m = m_ref[...]
        mn = jnp.maximum(m, jnp.max(s, axis=1, keepdims=True))
        c = jnp.exp(m - mn)
        p = jnp.exp(s - mn)
        pb = p.astype(BF16)
        pv = jnp.concatenate([jnp.dot(pb[h * hr:(h + 1) * hr], vs[h], preferred_element_type=F32)
                              for h in range(A_HEADS)], axis=0)
        m_ref[...] = mn
        l_ref[...] = l_ref[...] * c + jnp.sum(p, axis=1, keepdims=True)
        acc_ref[...] = acc_ref[...] * c + pv

    update([buf_ref[slot, h, :, 0:LANES].astype(BF16) for h in range(A_HEADS)],
           [buf_ref[slot, h, :, LANES:2 * LANES].astype(BF16) for h in range(A_HEADS)],
           st * _PAGES_PER_STEP * page, _PAGES_PER_STEP * page, None)

    @pl.when(st == pl.num_programs(1) - 1)
    def _():
        new = jnp.concatenate([new_ref[...], jnp.zeros((LANES - nq, new_ref.shape[1]), BF16)], axis=0)
        update([new[:, 2 * h * LANES:(2 * h + 1) * LANES] for h in range(A_HEADS)],
               [new[:, (2 * h + 1) * LANES:(2 * h + 2) * LANES] for h in range(A_HEADS)],
               past, LANES, lambda kp: kp <= qpos_i)
        o = acc_ref[...] / l_ref[...]
        lam = lam_ref[0]
        for h in range(A_HEADS):
            oh = o[h * hr:h * hr + nq] - lam * o[h * hr + nq:(h + 1) * hr]
            y = oh * lax.rsqrt(jnp.mean(oh * oh, axis=-1, keepdims=True) + NORM_EPS) * subln_ref[...]
            o_ref[:, h * A_VD:(h + 1) * A_VD] = (y * (1.0 - lam_init)).astype(BF16)


def _diff_attn_sample(aq, akvb, cache_a, page_table, lam, subln, db, nq, lam_init):
    n_phys, page = cache_a.shape[1], cache_a.shape[2]
    n_pages = page_table.shape[1]
    past = n_pages * page
    assert n_pages % _PAGES_PER_STEP == 0 and nq == 8
    n_rows = A_HEADS * 2 * nq
    q5 = aq.reshape(db, nq, A_HEADS, 2, A_DH).transpose(0, 2, 3, 1, 4)
    wq = jnp.einsum('bhcqd,cC->bhcqCd', q5, jnp.eye(2, dtype=BF16)).reshape(db, n_rows, 2 * A_DH)
    per_b = lambda r, c: pl.BlockSpec((None, r, c), lambda b, st, pt: (b, 0, 0))
    grid_spec = pltpu.PrefetchScalarGridSpec(
        num_scalar_prefetch=1,
        grid=(db, n_pages // _PAGES_PER_STEP),
        in_specs=[pl.BlockSpec(memory_space=pltpu.SMEM), pl.BlockSpec(memory_space=pltpu.SMEM),
                  pl.BlockSpec(memory_space=pl.ANY),
                  per_b(n_rows, 2 * A_DH), per_b(nq, 1024), pl.BlockSpec((1, A_VD), lambda b, st, pt: (0, 0))],
        out_specs=per_b(nq, 512),
        scratch_shapes=[pltpu.VMEM((n_rows, 1), F32), pltpu.VMEM((n_rows, 1), F32), pltpu.VMEM((n_rows, A_VD), F32),
                        pltpu.VMEM((2, A_HEADS, _PAGES_PER_STEP * page, 2 * LANES), F32),
                        pltpu.SemaphoreType.DMA((2,))],
    )
    out = pl.pallas_call(
        functools.partial(_diff_sample_kernel, nq=nq, past=past, page=page, lam_init=lam_init),
        grid_spec=grid_spec,
        out_shape=jax.ShapeDtypeStruct((db, nq, 512), BF16),
        compiler_params=_params(("arbitrary", "arbitrary")),
        name="diff_attn_sample",
    )(page_table, lam.reshape(1), jnp.asarray(_alibi(A_HEADS)), cache_a, wq,
      akvb.reshape(db, nq, 1024), subln.reshape(1, A_VD))
    return out.reshape(db * nq, 512)


def _compress_sample(cache_cmp, page_table, wbig, w2big, pec, db):
    n_phys, page = cache_cmp.shape[1], cache_cmp.shape[2]
    n_pages = page_table.shape[1]
    pps = math.gcd(n_pages, 32)
    assert page % (8 * CMP_STRIDE) == 0
    m = pps * page // CMP_STRIDE
    n_ch = n_pages * page // CMP_STRIDE
    del n_phys
    const = lambda shp: pl.BlockSpec(shp, lambda b, st, pt: (0, 0))
    grid_spec = pltpu.PrefetchScalarGridSpec(
        num_scalar_prefetch=1,
        grid=(db, n_pages // pps),
        in_specs=[pl.BlockSpec(memory_space=pl.ANY), const(wbig.shape), const(pec.shape), const(w2big.shape)],
        out_specs=pl.BlockSpec((None, m, 256), lambda b, st, pt: (b, st, 0)),
        scratch_shapes=[pltpu.VMEM((8, 512), F32), pltpu.VMEM((2, B_KV * pps, page, LANES), F32),
                        pltpu.SemaphoreType.DMA((2,))],
    )
    return pl.pallas_call(
        functools.partial(_compress_sample_kernel, n_in=pps, rows=page),
        grid_spec=grid_spec,
        out_shape=jax.ShapeDtypeStruct((db, n_ch, 256), BF16),
        compiler_params=_params(("arbitrary", "arbitrary")),
        name="nsa_compress_sample",
    )(page_table, cache_cmp, wbig, pec, w2big)


def _nsa_sample_kernel(pt_ref, slopes_ref, cache_ref, wq_ref, ckv_ref, win_ref, snew_ref, wnew_ref, gates_ref,
                       ovt_ref, onorm_ref, o_ref, neg_ref, oc_ref, m_ref, l_ref, acc_ref, buf_ref, sem,
                       *, nq, past, page, n_slc):
    def copies(b, p0, slot):
        return [pltpu.make_async_copy(cache_ref.at[0, pt_ref[b, p0 + i], :, g, :],
                                      buf_ref.at[slot, pl.ds(i * page, page), pl.ds(g * LANES, LANES)], sem.at[slot])
                for i in range(_PAGES_PER_STEP) for g in range(B_KV)]

    slot = _paged_fetch(pt_ref, sem, copies, _PAGES_PER_STEP)
    st = pl.program_id(1)
    n_rows = B_HEADS * nq
    n_j = neg_ref.shape[1]
    slope_col = _head_rows(n_rows, nq, slopes_ref, 0)
    rowi = lax.broadcasted_iota(I32, (n_rows, 1), 0)
    qpos_i = past + rowi % nq
    qpos = qpos_i.astype(F32)
    wq = wq_ref[...]
    init = (jnp.full((n_rows, 1), NEG, F32), jnp.zeros((n_rows, 1), F32), jnp.zeros((n_rows, 256), F32))

    def pad_rows(x):
        return jnp.concatenate([x, jnp.zeros((LANES - x.shape[0], x.shape[1]), x.dtype)], axis=0)

    def scores(rows_bf, kpos_i):
        return _nt(wq, rows_bf) - slope_col * (qpos - kpos_i.astype(F32))

    def block_bias(blk_of_key):
        n = blk_of_key.shape[1]
        onehot = (lax.broadcasted_iota(I32, (n_j, n), 0) == blk_of_key).astype(BF16)
        return jnp.dot(neg_ref[...], onehot, preferred_element_type=F32)

    @pl.when(st == 0)
    def _():
        ckv = ckv_ref[...]
        n_ch = ckv.shape[0]
        mcol = lax.broadcasted_iota(I32, (1, n_ch), 1)
        cpos = mcol * CMP_STRIDE + (CMP_STRIDE - 1)
        valid = (mcol >= 1) & (cpos <= qpos_i)
        s = jnp.where(valid, scores(ckv, cpos), NEG)
        p = jnp.exp(s - jnp.max(s, axis=1, keepdims=True))
        p = jnp.where(qpos_i >= CMP_LEN - 1, p / jnp.sum(p, axis=1, keepdims=True), 0.0)
        oc_ref[...] = jnp.dot(p.astype(BF16), ckv, preferred_element_type=F32)
        grp = B_REP * nq
        psum = jnp.concatenate(
            [sum(p[g * grp + r * nq:g * grp + (r + 1) * nq] for r in range(B_REP)) for g in range(B_KV)], axis=0)
        hi = psum.astype(BF16)
        lo = (psum - hi.astype(F32)).astype(BF16)
        imp_t = _nt(ovt_ref[...], hi) + _nt(ovt_ref[...], lo)
        shp = imp_t.shape
        jrow = lax.broadcasted_iota(I32, shp, 0)
        jcur = (past + lax.broadcasted_iota(I32, shp, 1) % nq) // SEL_BLOCK
        forced = (jrow == 0) | (jrow == jcur) | (jrow == jcur - 1)
        score = jnp.where(jrow > jcur, -FORCED, jnp.where(forced, FORCED, imp_t))
        rank = _rank_rows(score, n_slc)
        neg_t = jnp.where(rank < N_SEL, 0.0, NEG)
        neg = jnp.concatenate([neg_t, jnp.zeros((n_j, LANES - shp[1]), F32)], axis=1).T
        neg = jnp.concatenate([neg[g * nq:(g + 1) * nq] for g in range(B_KV) for _ in range(B_REP)], axis=0)
        neg_ref[...] = neg.astype(BF16)
        m_ref[...], l_ref[...], acc_ref[...] = init

    rows = buf_ref[slot].astype(BF16)
    n_keys = _PAGES_PER_STEP * page
    kpos_i = st * n_keys + lax.broadcasted_iota(I32, (1, n_keys), 1)
    m_ref[...], l_ref[...], acc_ref[...] = _online_update(
        (m_ref[...], l_ref[...], acc_ref[...]), scores(rows, kpos_i) + block_bias(kpos_i // SEL_BLOCK), rows)

    @pl.when(st == pl.num_programs(1) - 1)
    def _():
        npos_i = past + lax.broadcasted_iota(I32, (1, LANES), 1)
        causal = npos_i <= qpos_i
        snew = pad_rows(snew_ref[...].astype(BF16))
        s = jnp.where(causal, scores(snew, npos_i) + block_bias(npos_i // SEL_BLOCK), NEG)
        _, l_s, a_s = _online_update((m_ref[...], l_ref[...], acc_ref[...]), s, snew)
        win = win_ref[...].astype(BF16)
        wb = win.shape[0]
        wpos_i = past - wb + lax.broadcasted_iota(I32, (1, wb), 1)
        d = qpos_i - wpos_i
        cw = _online_update(init, jnp.where((d >= 0) & (d < WINDOW), scores(win, wpos_i), NEG), win)
        wnew = pad_rows(wnew_ref[...].astype(BF16))
        _, l_w, a_w = _online_update(cw, jnp.where(causal, scores(wnew, npos_i), NEG), wnew)
        gates = jnp.concatenate([gates_ref[...]] * B_HEADS, axis=0)
        glane = lax.broadcasted_iota(I32, (n_rows, LANES), 1)

        def gate_col(branch):
            return jnp.sum(jnp.where(glane == branch * B_HEADS + rowi // nq, gates, 0.0), axis=1, keepdims=True)

        o = gate_col(0) * oc_ref[...] + gate_col(1) * (a_s / l_s) + gate_col(2) * (a_w / l_w)
        lane = lax.broadcasted_iota(I32, (nq, LANES), 1)
        outs = []
        for hd in range(B_HEADS):
            g = hd // B_REP
            blk = jnp.where(lane >= B_DH, o[hd * nq:(hd + 1) * nq, g * LANES:(g + 1) * LANES], 0.0)
            y = blk * lax.rsqrt(jnp.sum(blk * blk, axis=1, keepdims=True) * (1.0 / B_DH) + NORM_EPS) * onorm_ref[...]
            outs.append(y[:, B_DH:])
        o_ref[...] = jnp.concatenate(outs, axis=1).astype(BF16)


def _overlap_rows(n_rows, n_slc, n_j):
    m = np.arange(n_rows)[None, :]
    start = (m - 1) * CMP_STRIDE
    j = np.arange(n_j)[:, None]
    ov = (start < (j + 1) * SEL_BLOCK) & (start + CMP_LEN > j * SEL_BLOCK) & (m >= 1) & (j < n_slc)
    return jnp.asarray(ov.astype(np.float32), BF16)


def _nsa_sample(bq, ckv, skv, wkv, gates, cache_cmp, cache_sel, state_win, page_table, cmp_pe, cmp_w1, cmp_w2,
                out_norm, db, nq):
    n_phys, page = cache_sel.shape[1], cache_sel.shape[2]
    n_pages = page_table.shape[1]
    past = n_pages * page
    wb = state_win.shape[2]
    assert nq == 8 and nq < CMP_STRIDE and past % SEL_BLOCK == 0 and wb == WINDOW and page % SEL_BLOCK == 0
    n_ch = past // CMP_STRIDE
    n_slc = -(-(past + nq) // SEL_BLOCK)
    n_j = -(-n_slc // LANES) * LANES
    n_rows = B_HEADS * nq
    wbig, w2big, pec = _prep_compress(cmp_pe, cmp_w1, cmp_w2, v_first=False)
    ckvc = _compress_sample(cache_cmp, page_table, wbig, w2big, pec, db)
    q = bq[:, :, B_DH:].reshape(B_KV, B_REP, db, nq, B_DH)
    wq = jnp.einsum('grbqd,gG->bgrqGd', q, jnp.eye(B_KV, dtype=BF16))
    wq = jnp.pad(wq, ((0, 0),) * 5 + ((0, B_DH),)).reshape(db, n_rows, 256)
    onorm = jnp.pad(out_norm, (B_DH, 0)).reshape(1, LANES)
    del n_phys
    per_b = lambda r, c: pl.BlockSpec((None, r, c), lambda b, st, pt: (b, 0, 0))
    const = lambda shp: pl.BlockSpec(shp, lambda b, st, pt: (0, 0))
    grid_spec = pltpu.PrefetchScalarGridSpec(
        num_scalar_prefetch=1,
        grid=(db, n_pages // _PAGES_PER_STEP),
        in_specs=[pl.BlockSpec(memory_space=pltpu.SMEM), pl.BlockSpec(memory_space=pl.ANY),
                  per_b(n_rows, 256), per_b(n_ch, 256), per_b(wb, 256), per_b(nq, 256), per_b(nq, 256),
                  per_b(nq, LANES), const((n_j, n_ch)), const((1, LANES))],
        out_specs=per_b(nq, 512),
        scratch_shapes=[pltpu.VMEM((n_rows, n_j), BF16), pltpu.VMEM((n_rows, 256), F32), pltpu.VMEM((n_rows, 1), F32),
                        pltpu.VMEM((n_rows, 1), F32), pltpu.VMEM((n_rows, 256), F32),
                        pltpu.VMEM((2, _PAGES_PER_STEP * page, B_KV * LANES), F32), pltpu.SemaphoreType.DMA((2,))],
    )
    out = pl.pallas_call(
        functools.partial(_nsa_sample_kernel, nq=nq, past=past, page=page, n_slc=n_slc),
        grid_spec=grid_spec,
        out_shape=jax.ShapeDtypeStruct((db, nq, 512), BF16),
        compiler_params=_params(("arbitrary", "arbitrary")),
        name="nsa_sample",
    )(page_table, jnp.asarray(_alibi(B_HEADS)), cache_sel, wq, ckvc,
      state_win.reshape(db, wb, 256), skv.reshape(db, nq, 256), wkv.reshape(db, nq, 256), gates.reshape(db, nq, LANES),
      _overlap_rows(n_ch, n_slc, n_j), onorm)
    return out.reshape(db * nq, 512)


def kernel(x_prompt, x_sample, cache_a, cache_cmp, cache_sel, state_win, page_table, norm_mix, w_in, diff_lambda,
           diff_subln, cmp_pe, cmp_w1, cmp_w2, nsa_out_norm, w_out, norm_ffn, peer_wq, peer_subkeys, peer_u, peer_v,
           norm_final):
    b, s, _ = x_prompt.shape
    db, nq, _ = x_sample.shape
    lam_init = 0.8 - 0.6 * math.exp(-0.3 * 0)
    dl = diff_lambda[0]
    lam = jnp.exp(jnp.sum(dl[0] * dl[1])) - jnp.exp(jnp.sum(dl[2] * dl[3])) + lam_init
    w = _prep_w_in(w_in[0])
    wbig, w2big, pec = _prep_compress(cmp_pe[0], cmp_w1[0], cmp_w2[0])
    u_tab = _pack_table(peer_u[0])
    v_tab = _pack_table(peer_v[0])

    ts = db * nq
    xs = x_sample.reshape(ts, D_MODEL)
    saq, sakv, sakvb, sbq, sckv, sskv, swkv, _, _, sgates = _in_proj(xs, norm_mix[0], w)
    soa = _diff_attn_sample(saq, sakvb, cache_a, page_table, lam, diff_subln[0], db, nq, lam_init)
    sob = _nsa_sample(sbq, sckv, sskv, swkv, sgates, cache_cmp, cache_sel, state_win, page_table, cmp_pe[0],
                      cmp_w1[0], cmp_w2[0], nsa_out_norm[0], db, nq)
    sx1, sxn, seid, sgate = _route(xs, soa, sob, w_out[0], norm_ffn[0], peer_wq[0], peer_subkeys[0])
    y_sample = _peer_experts(sx1, sxn, seid, sgate, u_tab, v_tab, norm_final).reshape(db, nq, D_MODEL)
    new_win = jnp.concatenate([state_win[0][:, nq:], swkv.reshape(db, nq, B_KV, 128)], axis=1)

    xp = x_prompt.reshape(b * s, D_MODEL)
    aq, akv, akvb, bq, ckv, skv, wkv, skvb, wkvb, gates = _in_proj(xp, norm_mix[0], w)
    oa = _diff_attn_prompt(aq, akvb, lam, diff_subln[0], b, s, lam_init)
    ckvc = _compress_prompt(ckv, b, s, wbig, w2big, pec)
    ob = _nsa_prompt(bq, ckvc, skvb, wkvb, gates, nsa_out_norm[0], b, s)
    x1, xn, eid, gate = _route(xp, oa, ob, w_out[0], norm_ffn[0], peer_wq[0], peer_subkeys[0])
    y_prompt = _peer_experts(x1, xn, eid, gate, u_tab, v_tab, norm_final).reshape(b, s, D_MODEL)
    win = min(WINDOW, s)
    outs_p = (akv.reshape(1, b, s, A_HEADS, 256), ckv.reshape(1, b, s, B_KV, 128), skv.reshape(1, b, s, B_KV, 128),
              wkv.reshape(b, s, B_KV, 128)[None, :, s - win:])
    return (y_prompt, y_sample, outs_p[0], sakv.reshape(1, db, nq, A_HEADS, 256), outs_p[1],
            sckv.reshape(1, db, nq, B_KV, 128), outs_p[2], sskv.reshape(1, db, nq, B_KV, 128), outs_p[3],
            new_win[None])
```
